```python
import math
import jax
import jax.numpy as jnp
from jax import lax
import numpy as np

D_MODEL = 2048
BATCH = 8
SEQ = 2048
DEPTH = 2
DEC_BATCH = 16
DEC_SEQ = 2048
PAST_LEN = 128

HEAD_DIM = 128
PLE_DIM = 256
GRID_W = 64
EPS = 1e-6
NEG = -1e30
ROPE_THETA = 500000.0
ROT_DIM = HEAD_DIM // 4
D_FF = ((8 * D_MODEL + 3 * 256 - 1) // (3 * 256)) * 256
GLA_HEADS = 4
GLA_DV = D_MODEL // (2 * GLA_HEADS)
GLA_DK = GLA_DV // 2
GLA_RANK = 16
GLA_TAU = 16.0
GLA_CHUNK = 64
NA_HEADS = D_MODEL // (2 * HEAD_DIM)
NA_WIN_ROWS = 8
NA_WIN_COLS = 16
DIL_PAIRS = ((128, 1), (512, 4), (2048, 16))
DIL_HEADS = D_MODEL // (2 * HEAD_DIM)
DIL_BLOCK = 64
DIFF_DH = 64
DIFF_HEADS = D_MODEL // (4 * DIFF_DH)
Q_BLOCK = 128

N_EVEN = (DEPTH + 1) // 2
N_ODD = DEPTH // 2
EVEN_SPLIT = (GLA_HEADS * GLA_DK, GLA_HEADS * GLA_DK, GLA_HEADS * GLA_DV, GLA_HEADS * GLA_DV, 2 * GLA_RANK,
              NA_HEADS * HEAD_DIM, NA_HEADS * HEAD_DIM, NA_HEADS * HEAD_DIM)
EVEN_WIDTH = sum(EVEN_SPLIT)
EVEN_OUT = GLA_HEADS * GLA_DV + NA_HEADS * HEAD_DIM
ODD_SPLIT = (len(DIL_PAIRS) * DIL_HEADS * HEAD_DIM, DIL_HEADS * HEAD_DIM, DIL_HEADS * HEAD_DIM,
             DIFF_HEADS * 2 * DIFF_DH, DIFF_HEADS * 2 * DIFF_DH, DIFF_HEADS * 2 * DIFF_DH)
ODD_WIDTH = sum(ODD_SPLIT)
ODD_OUT = DIL_HEADS * HEAD_DIM + DIFF_HEADS * 2 * DIFF_DH

kernel_name = 'hybrid_bidir_encoder'


def _rmsnorm(x, g):
    xf = x.astype(jnp.float32)
    y = xf * lax.rsqrt(jnp.mean(xf * xf, axis=-1, keepdims=True) + EPS)
    return (y * g.astype(jnp.float32)).astype(x.dtype)


def _split(z, sizes):
    outs, off = [], 0
    for s in sizes:
        outs.append(z[..., off:off + s])
        off += s
    return outs


def _rope(x, rot):
    T = x.shape[1]
    half = rot // 2
    inv = ROPE_THETA ** (-jnp.arange(half, dtype=jnp.float32) / half)
    ang = jnp.arange(T, dtype=jnp.float32)[:, None] * inv[None, :]
    shape = (1, T) + (1,) * (x.ndim - 3) + (half,)
    cos = jnp.cos(ang).reshape(shape)
    sin = jnp.sin(ang).reshape(shape)
    xf = x.astype(jnp.float32)
    x1, x2, rest = xf[..., :half], xf[..., half:rot], xf[..., rot:]
    return jnp.concatenate([x1 * cos - x2 * sin, x2 * cos + x1 * sin, rest], axis=-1).astype(x.dtype)


def _gla_scan(q, k, v, g, strict):
    B, H, T, dk = q.shape
    dv = v.shape[-1]
    L = GLA_CHUNK
    N = T // L
    q, k, g = (t.reshape(B, H, N, L, dk) for t in (q, k, g))
    v = v.reshape(B, H, N, L, dv)
    b = jnp.cumsum(g, axis=3)
    b_last = b[:, :, :, -1:]
    qd = q * jnp.exp(b)
    kd = k * jnp.exp(-b)
    causal = jnp.tril(jnp.ones((L, L), dtype=bool), -1 if strict else 0)
    att = jnp.where(causal, jnp.einsum('bhnid,bhnjd->bhnij', qd, kd), 0.0)
    intra = jnp.einsum('bhnij,bhnjv->bhniv', att, v)
    u = jnp.einsum('bhnjd,bhnjv->nbhdv', k * jnp.exp(b_last - b), v)
    decay = jnp.moveaxis(jnp.exp(b_last[:, :, :, 0]), 2, 0)

    def step(s, inp):
        d, uc = inp
        return d[..., None] * s + uc, s

    _, s_in = lax.scan(step, jnp.zeros((B, H, dk, dv), jnp.float32), (decay, u))
    inter = jnp.einsum('bhnid,nbhdv->bhniv', qd, s_in)
    return (intra + inter).reshape(B, H, T, dv)


def _gla(q, k, v, lr, wg_f, bg_f, wg_b, bg_b, g_norm):
    B, T, H, dk = q.shape
    dv = v.shape[-1]
    dt = v.dtype
    hf = lambda t: jnp.transpose(t, (0, 2, 1, 3)).astype(jnp.float32)
    gf = jax.nn.log_sigmoid((lr[..., :GLA_RANK] @ wg_f + bg_f).astype(jnp.float32)) / GLA_TAU
    gb = jax.nn.log_sigmoid((lr[..., GLA_RANK:] @ wg_b + bg_b).astype(jnp.float32)) / GLA_TAU
    qh = hf(q) * dk ** -0.5
    kh, vh = hf(k), hf(v)
    gfh, gbh = hf(gf.reshape(B, T, H, dk)), hf(gb.reshape(B, T, H, dk))
    flip = lambda t: jnp.flip(t, axis=2)
    o_f = _gla_scan(qh, kh, vh, gfh, False)
    o_b = flip(_gla_scan(flip(qh), flip(kh), flip(vh), flip(gbh), True))
    o = jnp.transpose(o_f + o_b, (0, 2, 1, 3))
    o = _rmsnorm(o, g_norm)
    return o.reshape(B, T, H * dv).astype(dt)


def _neighbourhood_attn(q, k, v, rpb):
    B, T, H, dh = q.shape
    rows = T // GRID_W
    wr = min(NA_WIN_ROWS, rows)
    grid = lambda t: t.reshape(B, rows, GRID_W, H, dh)
    qg, kg, vg = grid(q), grid(k), grid(v)
    r = jnp.arange(rows)
    key_rows = jnp.clip(r - wr // 2, 0, rows - wr)[:, None] + jnp.arange(wr)[None, :]
    kw = kg[:, key_rows]
    vw = vg[:, key_rows]
    c = jnp.arange(GRID_W)
    c0 = jnp.clip(c - NA_WIN_COLS // 2, 0, GRID_W - NA_WIN_COLS)
    col_ok = (c[None, :] >= c0[:, None]) & (c[None, :] < c0[:, None] + NA_WIN_COLS)
    dr = key_rows - r[:, None] + NA_WIN_ROWS - 1
    dc = jnp.clip(c[None, :] - c[:, None] + NA_WIN_COLS - 1, 0, 2 * NA_WIN_COLS - 2)
    bias = rpb[:, dr[:, None, :, None], dc[None, :, None, :]]
    s = jnp.einsum('brqhd,brokhd->bhrqok', qg, kw).astype(jnp.float32) * dh ** -0.5
    s = s + bias.astype(jnp.float32)[None]
    s = jnp.where(col_ok[:, None, :], s, NEG)
    p = jax.nn.softmax(s.reshape(B, H, rows, GRID_W, wr * GRID_W), axis=-1).reshape(s.shape).astype(v.dtype)
    o = jnp.einsum('bhrqok,brokhd->brqhd', p, vw)
    return o.reshape(B, T, H * dh)


def _dilated_group(q, k, v, dil, radius):
    B, T, H, dh = q.shape
    n = T // dil
    nb = -(-n // DIL_BLOCK)
    n_pad = nb * DIL_BLOCK
    X = B * dil
    by_res = lambda t: t.reshape(B, n, dil, H, dh).transpose(0, 2, 1, 3, 4).reshape(X, n, H, dh)
    qr = jnp.pad(by_res(q), ((0, 0), (0, n_pad - n), (0, 0), (0, 0)))
    kr = jnp.pad(by_res(k), ((0, 0), (DIL_BLOCK, n_pad - n + DIL_BLOCK), (0, 0), (0, 0)))
    vr = jnp.pad(by_res(v), ((0, 0), (DIL_BLOCK, n_pad - n + DIL_BLOCK), (0, 0), (0, 0)))
    qb = qr.reshape(X, nb, DIL_BLOCK, H, dh)
    kb = kr.reshape(X, nb + 2, DIL_BLOCK, H, dh)
    vb = vr.reshape(X, nb + 2, DIL_BLOCK, H, dh)
    band = lambda t: jnp.concatenate([t[:, :-2], t[:, 1:-1], t[:, 2:]], axis=2)
    kband, vband = band(kb), band(vb)
    qpos = jnp.arange(n_pad).reshape(nb, DIL_BLOCK)
    kpos = qpos[:, :1] - DIL_BLOCK + jnp.arange(3 * DIL_BLOCK)[None, :]
    kp = kpos[:, None, :]
    valid = (jnp.abs(kp - qpos[:, :, None]) <= radius) & (kp >= 0) & (kp < n)
    s = jnp.einsum('xnqhd,xnkhd->xhnqk', qb, kband).astype(jnp.float32)
    s = jnp.where(valid, s, NEG)
    lse = jax.nn.logsumexp(s, axis=-1, keepdims=True)
    p = jnp.exp(s - lse).astype(v.dtype)
    o = jnp.einsum('xhnqk,xnkhd->xnqhd', p, vband).reshape(X, n_pad, H, dh)[:, :n]
    lse = jnp.transpose(lse[..., 0], (0, 2, 3, 1)).reshape(X, n_pad, H)[:, :n]
    undo = lambda y: y.reshape((B, dil, n) + y.shape[2:]).swapaxes(1, 2).reshape((B, T) + y.shape[2:])
    return undo(o), undo(lse)


def _dilated_attn(q, k, v):
    B, T, G, H, dh = q.shape
    q = _rope(q, ROT_DIM) * dh ** -0.5
    k = _rope(k, ROT_DIM)
    outs, lses = [], []
    for g, (win, dil) in enumerate(DIL_PAIRS):
        o, l = _dilated_group(q[:, :, g], k, v, dil, win // (2 * dil))
        outs.append(o)
        lses.append(l)
    w = jax.nn.softmax(jnp.stack(lses, axis=0), axis=0).astype(v.dtype)
    o = jnp.sum(w[..., None] * jnp.stack(outs, axis=0), axis=0)
    return o.reshape(B, T, H * dh)


def _diff_attn(q, k, v, lam, subln_g, lam_init):
    B, T, H, _, dh = q.shape
    q = _rope(q, dh // 4) * dh ** -0.5
    k = _rope(k, dh // 4)
    lamf = lam.astype(jnp.float32)
    lam_full = jnp.exp(jnp.sum(lamf[0] * lamf[1])) - jnp.exp(jnp.sum(lamf[2] * lamf[3])) + lam_init
    nq = T // Q_BLOCK
    qb = jnp.moveaxis(q.reshape(B, nq, Q_BLOCK, H, 2, dh), 1, 0)

    def block(qc):
        s = jnp.einsum('bqhcd,bkhcd->bhcqk', qc, k).astype(jnp.float32)
        p = jax.nn.softmax(s, axis=-1)
        a = (p[:, :, 0] - lam_full * p[:, :, 1]).astype(v.dtype)
        return jnp.einsum('bhqk,bkhe->bqhe', a, v)

    o = jnp.moveaxis(lax.map(block, qb), 0, 1).reshape(B, T, H, 2 * dh)
    o = _rmsnorm(o, subln_g) * (1.0 - lam_init)
    return o.reshape(B, T, H * 2 * dh)


def _mixer_even(a, w_in, w_out, wg_f, bg_f, wg_b, bg_b, gla_g, rpb):
    B, T, _ = a.shape
    qa, ka, va, ra, lr, qb, kb, vb = _split(a @ w_in, EVEN_SPLIT)
    ya = _gla(qa.reshape(B, T, GLA_HEADS, GLA_DK), ka.reshape(B, T, GLA_HEADS, GLA_DK),
              va.reshape(B, T, GLA_HEADS, GLA_DV), lr, wg_f, bg_f, wg_b, bg_b, gla_g)
    ya = ya * jax.nn.silu(ra)
    nh = lambda t: t.reshape(B, T, NA_HEADS, HEAD_DIM)
    yb = _neighbourhood_attn(nh(qb), nh(kb), nh(vb), rpb)
    return jnp.concatenate([ya, yb], axis=-1) @ w_out


def _mixer_odd(a, w_in, w_out, lam, subln_g, lam_init):
    B, T, _ = a.shape
    qc, kc, vc, qd, kd, vd = _split(a @ w_in, ODD_SPLIT)
    yc = _dilated_attn(qc.reshape(B, T, len(DIL_PAIRS), DIL_HEADS, HEAD_DIM),
                       kc.reshape(B, T, DIL_HEADS, HEAD_DIM), vc.reshape(B, T, DIL_HEADS, HEAD_DIM))
    yd = _diff_attn(qd.reshape(B, T, DIFF_HEADS, 2, DIFF_DH), kd.reshape(B, T, DIFF_HEADS, 2, DIFF_DH),
                    vd.reshape(B, T, DIFF_HEADS, 2 * DIFF_DH), lam, subln_g, lam_init)
    return jnp.concatenate([yc, yd], axis=-1) @ w_out


def _swiglu(a, wg, wu, wd):
    return (jax.nn.silu(a @ wg) * (a @ wu)) @ wd


def _trunk(x, p, norm_g, w_in_even, w_out_even, gla_wg_fwd, gla_bg_fwd, gla_wg_bwd, gla_bg_bwd,
           gla_norm_g, na_rpb, w_in_odd, w_out_odd, diff_lambda, diff_subln_g,
           w_ffn_gate, w_ffn_up, w_ffn_down, w_ple_proj, w_ple_gate):
    h = x
    for i in range(DEPTH):
        g = norm_g[i]
        a = _rmsnorm(h, g[0])
        j = i // 2
        if i % 2 == 0:
            m = _mixer_even(a, w_in_even[j], w_out_even[j], gla_wg_fwd[j], gla_bg_fwd[j],
                            gla_wg_bwd[j], gla_bg_bwd[j], gla_norm_g[j], na_rpb[j])
        else:
            m = _mixer_odd(a, w_in_odd[j], w_out_odd[j], diff_lambda[j], diff_subln_g[j],
                           0.8 - 0.6 * math.exp(-0.3 * i))
        h = h + _rmsnorm(m, g[1])
        f = _swiglu(_rmsnorm(h, g[2]), w_ffn_gate[i], w_ffn_up[i], w_ffn_down[i])
        h = h + _rmsnorm(f, g[3])
        e = _rmsnorm(p[i] @ w_ple_proj[i], g[4])
        h = h + e * jax.nn.sigmoid(h @ w_ple_gate[i])
    return h


def setup_inputs(seed: int = 0) -> dict:
    key = jax.random.key(seed)
    ks = jax.random.split(key, 24)
    nrm = lambda k, shape, scale: jax.random.normal(k, shape, jnp.float32) * scale
    D = D_MODEL
    return {
        'x_prompt': nrm(ks[0], (BATCH, SEQ, D), 1.0),
        'x_sample': nrm(ks[1], (DEC_BATCH, DEC_SEQ, D), 1.0),
        'p_prompt': nrm(ks[2], (DEPTH, BATCH, SEQ, PLE_DIM), 1.0),
        'p_sample': nrm(ks[3], (DEPTH, DEC_BATCH, DEC_SEQ, PLE_DIM), 1.0),
        'norm_g': 1.0 + nrm(ks[4], (DEPTH, 5, D), 0.05),
        'w_in_even': nrm(ks[5], (N_EVEN, D, EVEN_WIDTH), D ** -0.5),
        'w_out_even': nrm(ks[6], (N_EVEN, EVEN_OUT, D), EVEN_OUT ** -0.5),
        'gla_wg_fwd': nrm(ks[7], (N_EVEN, GLA_RANK, GLA_HEADS * GLA_DK), GLA_RANK ** -0.5),
        'gla_bg_fwd': nrm(ks[8], (N_EVEN, GLA_HEADS * GLA_DK), 0.1),
        'gla_wg_bwd': nrm(ks[9], (N_EVEN, GLA_RANK, GLA_HEADS * GLA_DK), GLA_RANK ** -0.5),
        'gla_bg_bwd': nrm(ks[10], (N_EVEN, GLA_HEADS * GLA_DK), 0.1),
        'gla_norm_g': 1.0 + nrm(ks[11], (N_EVEN, GLA_DV), 0.05),
        'na_rpb': nrm(ks[12], (N_EVEN, NA_HEADS, 2 * NA_WIN_ROWS - 1, 2 * NA_WIN_COLS - 1), 0.1),
        'w_in_odd': nrm(ks[13], (N_ODD, D, ODD_WIDTH), D ** -0.5),
        'w_out_odd': nrm(ks[14], (N_ODD, ODD_OUT, D), ODD_OUT ** -0.5),
        'diff_lambda': nrm(ks[15], (N_ODD, 4, DIFF_DH), 0.1),
        'diff_subln_g': 1.0 + nrm(ks[16], (N_ODD, 2 * DIFF_DH), 0.05),
        'w_ffn_gate': nrm(ks[17], (DEPTH, D, D_FF), D ** -0.5),
        'w_ffn_up': nrm(ks[18], (DEPTH, D, D_FF), D ** -0.5),
        'w_ffn_down': nrm(ks[19], (DEPTH, D_FF, D), D_FF ** -0.5),
        'w_ple_proj': nrm(ks[20], (DEPTH, PLE_DIM, D), PLE_DIM ** -0.5),
        'w_ple_gate': nrm(ks[21], (DEPTH, D, D), D ** -0.5),
    }


def reference(x_prompt, x_sample, p_prompt, p_sample, norm_g, w_in_even, w_out_even, gla_wg_fwd, gla_bg_fwd,
              gla_wg_bwd, gla_bg_bwd, gla_norm_g, na_rpb, w_in_odd, w_out_odd, diff_lambda, diff_subln_g,
              w_ffn_gate, w_ffn_up, w_ffn_down, w_ple_proj, w_ple_gate):
    y_prompt = _trunk(x_prompt, p_prompt, norm_g, w_in_even, w_out_even, gla_wg_fwd, gla_bg_fwd, gla_wg_bwd,
                      gla_bg_bwd, gla_norm_g, na_rpb, w_in_odd, w_out_odd, diff_lambda, diff_subln_g,
                      w_ffn_gate, w_ffn_up, w_ffn_down, w_ple_proj, w_ple_gate)
    y_sample = _trunk(x_sample, p_sample, norm_g, w_in_even, w_out_even, gla_wg_fwd, gla_bg_fwd, gla_wg_bwd,
                      gla_bg_bwd, gla_norm_g, na_rpb, w_in_odd, w_out_odd, diff_lambda, diff_subln_g,
                      w_ffn_gate, w_ffn_up, w_ffn_down, w_ple_proj, w_ple_gate)
    return (y_prompt, y_sample)
```

```python
import functools
import math

import jax
import jax.numpy as jnp
from jax import lax
from jax.experimental import pallas as pl
from jax.experimental.pallas import tpu as pltpu

F32 = jnp.float32
BF16 = jnp.bfloat16

D_MODEL = 2048
DEPTH = 2
HEAD_DIM = 128
PLE_DIM = 256
GRID_W = 64
EPS = 1e-6
NEG = -1e30
ROPE_THETA = 500000.0
ROT_DIM = HEAD_DIM // 4
D_FF = 5632
GLA_HEADS = 4
GLA_DV = 256
GLA_DK = 128
GLA_RANK = 16
GLA_TAU = 16.0
GLA_CHUNK = 64
NA_HEADS = 8
NA_WIN_ROWS = 8
NA_WIN_COLS = 16
DIL_PAIRS = ((128, 1), (512, 4), (2048, 16))
DIL_HEADS = 8
DIFF_DH = 64
DIFF_HEADS = 8

LANES = 128
VMEM_LIMIT = 56 * 1024 * 1024

TOKEN_TILE = 512
FFN_TILE = 512
IN_PROJ_TILE = 1024
ATTN_Q_BLOCK = 256

NT_DIMS = (((1,), (1,)), ((), ()))
TN_DIMS = (((0,), (0,)), ((), ()))


def _params(*semantics):
    return pltpu.CompilerParams(dimension_semantics=semantics, vmem_limit_bytes=VMEM_LIMIT)


def _divisor_tile(n, cap):
    return max(t for t in range(LANES, cap + 1, LANES) if n % t == 0)


def _rms(x, g):
    return x * lax.rsqrt(jnp.mean(x * x, axis=-1, keepdims=True) + EPS) * g


def _sigmoid(x):
    return 1.0 / (1.0 + jnp.exp(-x))


def _dot(a, b):
    return jnp.dot(a, b, preferred_element_type=F32)


def _dot_nt(a, b):
    return lax.dot_general(a, b, NT_DIMS, preferred_element_type=F32)


def _norm_matmul_kernel(x_ref, g_ref, w_ref, o_ref, a_ref):
    @pl.when(pl.program_id(1) == 0)
    def _():
        a_ref[...] = _rms(x_ref[...], g_ref[...]).astype(BF16)

    o_ref[...] = _dot(a_ref[...], w_ref[...]).astype(o_ref.dtype)


def _norm_matmul(x, g, w, out_dtype):
    M, D = x.shape
    N = w.shape[1]
    tm, tn = TOKEN_TILE, _divisor_tile(N, IN_PROJ_TILE)
    assert M % tm == 0
    return pl.pallas_call(
        _norm_matmul_kernel,
        grid=(M // tm, N // tn),
        in_specs=[
            pl.BlockSpec((tm, D), lambda i, j: (i, 0)),
            pl.BlockSpec((1, D), lambda i, j: (0, 0)),
            pl.BlockSpec((D, tn), lambda i, j: (0, j)),
        ],
        out_specs=pl.BlockSpec((tm, tn), lambda i, j: (i, j)),
        out_shape=jax.ShapeDtypeStruct((M, N), out_dtype),
        scratch_shapes=[pltpu.VMEM((tm, D), BF16)],
        compiler_params=_params("parallel", "arbitrary"),
        name="norm_in_proj",
    )(x, g.reshape(1, D), w)


def _out_proj_kernel(h_ref, ya_ref, yb_ref, g_ref, wa_ref, wb_ref, o_ref):
    m = _dot(ya_ref[...], wa_ref[...]) + _dot(yb_ref[...], wb_ref[...])
    o_ref[...] = h_ref[...] + _rms(m, g_ref[...])


def _out_proj(h, ya, yb, g, wa, wb):
    M, D = h.shape
    Ka, Kb = ya.shape[1], yb.shape[1]
    tm = TOKEN_TILE
    return pl.pallas_call(
        _out_proj_kernel,
        grid=(M // tm,),
        in_specs=[
            pl.BlockSpec((tm, D), lambda i: (i, 0)),
            pl.BlockSpec((tm, Ka), lambda i: (i, 0)),
            pl.BlockSpec((tm, Kb), lambda i: (i, 0)),
            pl.BlockSpec((1, D), lambda i: (0, 0)),
            pl.BlockSpec((Ka, D), lambda i: (0, 0)),
            pl.BlockSpec((Kb, D), lambda i: (0, 0)),
        ],
        out_specs=pl.BlockSpec((tm, D), lambda i: (i, 0)),
        out_shape=jax.ShapeDtypeStruct((M, D), F32),
        compiler_params=_params("parallel"),
        name="out_proj",
    )(h, ya, yb, g.reshape(1, D), wa, wb)


def _ffn_kernel(h_ref, g2_ref, g3_ref, wg_ref, wu_ref, wd_ref, o_ref, a_ref, acc_ref):
    f = pl.program_id(1)

    @pl.when(f == 0)
    def _():
        a_ref[...] = _rms(h_ref[...], g2_ref[...]).astype(BF16)
        acc_ref[...] = jnp.zeros_like(acc_ref)

    a = a_ref[...]
    gate = _dot(a, wg_ref[...])
    up = _dot(a, wu_ref[...])
    act = (gate * _sigmoid(gate) * up).astype(BF16)
    acc_ref[...] += _dot(act, wd_ref[...])

    @pl.when(f == pl.num_programs(1) - 1)
    def _():
        o_ref[...] = h_ref[...] + _rms(acc_ref[...], g3_ref[...])


def _ffn(h, g2, g3, wg, wu, wd):
    M, D = h.shape
    F = wg.shape[1]
    tm, tf = TOKEN_TILE, FFN_TILE
    assert M % tm == 0 and F % tf == 0
    return pl.pallas_call(
        _ffn_kernel,
        grid=(M // tm, F // tf),
        in_specs=[
            pl.BlockSpec((tm, D), lambda i, f: (i, 0)),
            pl.BlockSpec((1, D), lambda i, f: (0, 0)),
            pl.BlockSpec((1, D), lambda i, f: (0, 0)),
            pl.BlockSpec((D, tf), lambda i, f: (0, f)),
            pl.BlockSpec((D, tf), lambda i, f: (0, f)),
            pl.BlockSpec((tf, D), lambda i, f: (f, 0)),
        ],
        out_specs=pl.BlockSpec((tm, D), lambda i, f: (i, 0)),
        out_shape=jax.ShapeDtypeStruct((M, D), F32),
        scratch_shapes=[pltpu.VMEM((tm, D), BF16), pltpu.VMEM((tm, D), F32)],
        compiler_params=_params("parallel", "arbitrary"),
        name="ffn",
    )(h, g2.reshape(1, D), g3.reshape(1, D), wg, wu, wd)


def _ple_kernel(h_ref, p_ref, g_ref, wp_ref, wgate_ref, o_ref):
    h = h_ref[...]
    e = _rms(_dot(p_ref[...].astype(BF16), wp_ref[...]), g_ref[...])
    gate = _sigmoid(_dot(h.astype(BF16), wgate_ref[...]))
    o_ref[...] = h + e * gate


def _ple(h, p, g, wp, wgate):
    M, D = h.shape
    P = p.shape[1]
    tm = TOKEN_TILE
    return pl.pallas_call(
        _ple_kernel,
        grid=(M // tm,),
        in_specs=[
            pl.BlockSpec((tm, D), lambda i: (i, 0)),
            pl.BlockSpec((tm, P), lambda i: (i, 0)),
            pl.BlockSpec((1, D), lambda i: (0, 0)),
            pl.BlockSpec((P, D), lambda i: (0, 0)),
            pl.BlockSpec((D, D), lambda i: (0, 0)),
        ],
        out_specs=pl.BlockSpec((tm, D), lambda i: (i, 0)),
        out_shape=jax.ShapeDtypeStruct((M, D), F32),
        compiler_params=_params("parallel"),
        name="ple",
    )(h, p, g.reshape(1, D), wp, wgate)


def _cumsum_rows(tri, g):
    g1 = g.astype(BF16)
    r1 = g - g1.astype(F32)
    g2 = r1.astype(BF16)
    g3 = (r1 - g2.astype(F32)).astype(BF16)
    return _dot(tri, g1) + _dot(tri, g2) + _dot(tri, g3)


def _log_sigmoid(x):
    return jnp.minimum(x, 0.0) - jnp.log1p(jnp.exp(-jnp.abs(x)))


def _gla_kernel(q_ref, k_ref, v_ref, r_ref, lr_ref, wgf_ref, bgf_ref, wgb_ref, bgb_ref, gn_ref, o_ref,
                gf_ref, gb_ref, of_ref, ob_ref, sf_ref, sb_ref):
    T = q_ref.shape[0]
    L = GLA_CHUNK
    N = T // L
    lr = lr_ref[...].astype(BF16)
    gf_ref[...] = _log_sigmoid(_dot(lr, wgf_ref[...]) + bgf_ref[...]) / GLA_TAU
    gb_ref[...] = _log_sigmoid(_dot(lr, wgb_ref[...]) + bgb_ref[...]) / GLA_TAU
    sf_ref[...] = jnp.zeros_like(sf_ref)
    sb_ref[...] = jnp.zeros_like(sb_ref)

    row = lax.broadcasted_iota(jnp.int32, (L, L), 0)
    col = lax.broadcasted_iota(jnp.int32, (L, L), 1)
    scale = GLA_DK ** -0.5

    def one_direction(n, g_ref, cum_mask, att_mask, last_row, s_ref, out_ref):
        rows = pl.ds(pl.multiple_of(n * L, L), L)
        b = _cumsum_rows(cum_mask.astype(BF16), g_ref[rows, :])
        k = k_ref[rows, :]
        v = v_ref[rows, :].astype(BF16)
        qd = (q_ref[rows, :] * scale * jnp.exp(b)).astype(BF16)
        kd = (k * jnp.exp(-b)).astype(BF16)
        att = jnp.where(att_mask, _dot_nt(qd, kd), 0.0).astype(BF16)
        state = s_ref[...]
        out_ref[rows, :] = _dot(att, v) + _dot_nt(qd, state.astype(BF16))
        b_last = b[last_row:last_row + 1, :]
        ku = (k * jnp.exp(b_last - b)).astype(BF16)
        s_ref[...] = state * jnp.exp(b_last) + lax.dot_general(v, ku, TN_DIMS, preferred_element_type=F32)

    def body(n, carry):
        one_direction(n, gf_ref, col <= row, col <= row, L - 1, sf_ref, of_ref)
        one_direction(N - 1 - n, gb_ref, col >= row, col > row, 0, sb_ref, ob_ref)
        return carry

    lax.fori_loop(0, N, body, 0)

    r = r_ref[...]
    o = _rms(of_ref[...] + ob_ref[...], gn_ref[...])
    o_ref[...] = (o * (r * _sigmoid(r))).astype(o_ref.dtype)


def _gla(z, wgf, bgf, wgb, bgb, gn, cols):
    B, T, _ = z.shape
    dk, dv = GLA_DK, GLA_DV
    q0, k0, v0, r0, lr0 = cols
    zspec = lambda width, off: pl.BlockSpec((None, T, width), lambda b, h: (b, 0, off // width + h))
    return pl.pallas_call(
        _gla_kernel,
        grid=(B, GLA_HEADS),
        in_specs=[
            zspec(dk, q0), zspec(dk, k0), zspec(dv, v0), zspec(dv, r0),
            pl.BlockSpec((None, T, LANES), lambda b, h: (b, 0, lr0 // LANES)),
            pl.BlockSpec((LANES, dk), lambda b, h: (0, h)),
            pl.BlockSpec((1, dk), lambda b, h: (0, h)),
            pl.BlockSpec((LANES, dk), lambda b, h: (0, h)),
            pl.BlockSpec((1, dk), lambda b, h: (0, h)),
            pl.BlockSpec((1, dv), lambda b, h: (0, 0)),
        ],
        out_specs=pl.BlockSpec((None, T, dv), lambda b, h: (b, 0, h)),
        out_shape=jax.ShapeDtypeStruct((B, T, GLA_HEADS * dv), BF16),
        scratch_shapes=[
            pltpu.VMEM((T, dk), F32), pltpu.VMEM((T, dk), F32),
            pltpu.VMEM((T, dv), F32), pltpu.VMEM((T, dv), F32),
            pltpu.VMEM((dv, dk), F32), pltpu.VMEM((dv, dk), F32),
        ],
        compiler_params=_params("parallel", "arbitrary"),
        name="gla",
    )(z, z, z, z, z, wgf, bgf, wgb, bgb, gn)


def _na_kernel(q_ref, k_ref, v_ref, bias_ref, o_ref, kb_ref, vb_ref):
    T = q_ref.shape[0]
    rows = T // GRID_W
    wr = NA_WIN_ROWS
    kw_len = wr * GRID_W
    kb_ref[...] = k_ref[...].astype(BF16)
    vb_ref[...] = v_ref[...].astype(BF16)
    scale = HEAD_DIM ** -0.5

    def body(r, carry):
        kr0 = jnp.clip(r - wr // 2, 0, rows - wr)
        qrows = pl.ds(pl.multiple_of(r * GRID_W, GRID_W), GRID_W)
        krows = pl.ds(pl.multiple_of(kr0 * GRID_W, GRID_W), kw_len)
        s = _dot_nt(q_ref[qrows, :].astype(BF16), kb_ref[krows, :]) * scale + bias_ref[r - kr0]
        e = jnp.exp(s - jnp.max(s, axis=-1, keepdims=True))
        p = (e / jnp.sum(e, axis=-1, keepdims=True)).astype(BF16)
        o_ref[qrows, :] = _dot(p, vb_ref[krows, :]).astype(o_ref.dtype)
        return carry

    lax.fori_loop(0, rows, body, 0)


def _na(z, bias, cols):
    B, T, _ = z.shape
    assert T // GRID_W >= NA_WIN_ROWS
    q0, k0, v0 = cols
    zspec = lambda off: pl.BlockSpec((None, T, HEAD_DIM), lambda b, h: (b, 0, off // HEAD_DIM + h))
    return pl.pallas_call(
        _na_kernel,
        grid=(B, NA_HEADS),
        in_specs=[
            zspec(q0), zspec(k0), zspec(v0),
            pl.BlockSpec((None,) + bias.shape[1:], lambda b, h: (h, 0, 0, 0)),
        ],
        out_specs=pl.BlockSpec((None, T, HEAD_DIM), lambda b, h: (b, 0, h)),
        out_shape=jax.ShapeDtypeStruct((B, T, NA_HEADS * HEAD_DIM), BF16),
        scratch_shapes=[pltpu.VMEM((T, HEAD_DIM), BF16), pltpu.VMEM((T, HEAD_DIM), BF16)],
        compiler_params=_params("parallel", "arbitrary"),
        name="neighbourhood_attn",
    )(z, z, z, bias)


def _na_bias_table(rpb):
    wr, W = NA_WIN_ROWS, GRID_W
    o = jnp.arange(wr)
    d = jnp.arange(wr)
    dr = o[None, :] - d[:, None] + NA_WIN_ROWS - 1
    c = jnp.arange(W)
    dc = jnp.clip(c[None, :] - c[:, None] + NA_WIN_COLS - 1, 0, 2 * NA_WIN_COLS - 2)
    c0 = jnp.clip(c - NA_WIN_COLS // 2, 0, W - NA_WIN_COLS)
    col_ok = (c[None, :] >= c0[:, None]) & (c[None, :] < c0[:, None] + NA_WIN_COLS)
    bias = rpb[:, dr[:, None, :, None], dc[None, :, None, :]]
    bias = jnp.where(col_ok[None, None, :, None, :], bias.astype(F32), NEG)
    return bias.reshape(rpb.shape[0], wr, W, wr * W)


def _rope_tables(T, rot, period):
    half = rot // 2
    inv = ROPE_THETA ** (-jnp.arange(half, dtype=F32) / half)
    ang = jnp.arange(T, dtype=F32)[:, None] * inv[None, :]
    cos, sin = jnp.cos(ang), jnp.sin(ang)
    zeros = jnp.zeros((T, half), F32)
    rest0 = jnp.zeros((T, period - rot), F32)
    c = jnp.concatenate([cos, cos, jnp.ones((T, period - rot), F32)], axis=1)
    sa = jnp.concatenate([zeros, sin, rest0], axis=1)
    sb = jnp.concatenate([-sin, zeros, rest0], axis=1)
    rep = LANES // period
    return tuple(jnp.tile(t, (1, rep)) for t in (c, sa, sb))


def _rope(x, c, sa, sb, half):
    return x * c + pltpu.roll(x, half, 1) * sa + pltpu.roll(x, LANES - half, 1) * sb


def _class_rows(x, n, dil):
    return pl.ds(x, n) if dil == 1 else pl.ds(x, n, stride=dil)


def _dil_kernel(q0_ref, q1_ref, q2_ref, k_ref, v_ref, c_ref, sa_ref, sb_ref, o_ref,
                kr_ref, qr_ref, qd_ref, kd_ref, vd_ref, og_ref, lg_ref, on_ref, ln_ref):
    T = k_ref.shape[0]
    half = ROT_DIM // 2
    rope = lambda x: _rope(x, c_ref[...], sa_ref[...], sb_ref[...], half)
    kr_ref[...] = rope(k_ref[...])
    scale = HEAD_DIM ** -0.5

    for g, ((win, dil), qg_ref) in enumerate(zip(DIL_PAIRS, (q0_ref, q1_ref, q2_ref))):
        radius = win // (2 * dil)
        n = T // dil
        qb = min(ATTN_Q_BLOCK, n)
        kw = min(n, qb + 2 * radius)
        qr_ref[...] = rope(qg_ref[...]) * scale
        for x in range(dil):
            src = _class_rows(x, n, dil)
            dst = pl.ds(x * n, n)
            qd_ref[dst, :] = qr_ref[src, :].astype(BF16)
            kd_ref[dst, :] = kr_ref[src, :].astype(BF16)
            vd_ref[dst, :] = v_ref[src, :].astype(BF16)

        rel = (lax.broadcasted_iota(jnp.int32, (qb, kw), 1) - lax.broadcasted_iota(jnp.int32, (qb, kw), 0))

        def body(blk, carry, n=n, qb=qb, kw=kw, radius=radius, rel=rel):
            q_start = blk * qb
            seg = (q_start // n) * n
            k_start = jnp.clip(q_start - radius, seg, seg + n - kw)
            qrows = pl.ds(pl.multiple_of(q_start, qb), qb)
            krows = pl.ds(pl.multiple_of(k_start, GLA_CHUNK), kw)
            s = _dot_nt(qd_ref[qrows, :], kd_ref[krows, :])
            s = jnp.where(jnp.abs(rel + (k_start - q_start)) <= radius, s, NEG)
            m = jnp.max(s, axis=-1, keepdims=True)
            e = jnp.exp(s - m)
            l = jnp.sum(e, axis=-1, keepdims=True)
            og_ref[qrows, :] = _dot((e / l).astype(BF16), vd_ref[krows, :])
            lg_ref[qrows, :] = jnp.broadcast_to(m + jnp.log(l), (qb, LANES))
            return carry

        lax.fori_loop(0, T // qb, body, 0)

        for x in range(dil):
            src = pl.ds(x * n, n)
            dst = _class_rows(x, n, dil)
            on_ref[g, dst, :] = og_ref[src, :]
            ln_ref[g, dst, :] = lg_ref[src, :]

    l0, l1, l2 = ln_ref[0], ln_ref[1], ln_ref[2]
    m = jnp.maximum(jnp.maximum(l0, l1), l2)
    e0, e1, e2 = jnp.exp(l0 - m), jnp.exp(l1 - m), jnp.exp(l2 - m)
    den = e0 + e1 + e2
    o = (e0 / den) * on_ref[0] + (e1 / den) * on_ref[1] + (e2 / den) * on_ref[2]
    o_ref[...] = o.astype(o_ref.dtype)


def _dilated(z, tables, cols):
    B, T, _ = z.shape
    q0, k0, v0 = cols
    G = len(DIL_PAIRS)
    assert all(T % dil == 0 and (T // dil) % GLA_CHUNK == 0 for _, dil in DIL_PAIRS)
    zspec = lambda off: pl.BlockSpec((None, T, HEAD_DIM), lambda b, h: (b, 0, off // HEAD_DIM + h))
    tspec = pl.BlockSpec((T, LANES), lambda b, h: (0, 0))
    return pl.pallas_call(
        _dil_kernel,
        grid=(B, DIL_HEADS),
        in_specs=[zspec(q0 + g * DIL_HEADS * HEAD_DIM) for g in range(G)] + [zspec(k0), zspec(v0), tspec, tspec, tspec],
        out_specs=pl.BlockSpec((None, T, HEAD_DIM), lambda b, h: (b, 0, h)),
        out_shape=jax.ShapeDtypeStruct((B, T, DIL_HEADS * HEAD_DIM), BF16),
        scratch_shapes=[
            pltpu.VMEM((T, HEAD_DIM), F32), pltpu.VMEM((T, HEAD_DIM), F32),
            pltpu.VMEM((T, HEAD_DIM), BF16), pltpu.VMEM((T, HEAD_DIM), BF16), pltpu.VMEM((T, HEAD_DIM), BF16),
            pltpu.VMEM((T, HEAD_DIM), F32), pltpu.VMEM((T, LANES), F32),
            pltpu.VMEM((G, T, HEAD_DIM), F32), pltpu.VMEM((G, T, LANES), F32),
        ],
        compiler_params=_params("parallel", "arbitrary"),
        name="dilated_attn",
    )(z, z, z, z, z, *tables)


def _diff_kernel(q_ref, k_ref, v_ref, c_ref, sa_ref, sb_ref, lam_ref, sg_ref, o_ref, qb_ref, kb_ref, vb_ref,
                 *, lam_init):
    T = q_ref.shape[0]
    half = DIFF_DH // 8
    rope = lambda x: _rope(x, c_ref[...], sa_ref[...], sb_ref[...], half)
    qb_ref[...] = (rope(q_ref[...]) * DIFF_DH ** -0.5).astype(BF16)
    kb_ref[...] = rope(k_ref[...]).astype(BF16)
    vb_ref[...] = v_ref[...].astype(BF16)
    lam = lam_ref[...]
    lam_full = (jnp.exp(jnp.sum(lam[0:1] * lam[1:2], axis=-1, keepdims=True))
                - jnp.exp(jnp.sum(lam[2:3] * lam[3:4], axis=-1, keepdims=True)) + lam_init)
    first = lax.broadcasted_iota(jnp.int32, (1, LANES), 1) < DIFF_DH
    qblk = ATTN_Q_BLOCK

    def softmax_rows(q):
        s = _dot_nt(q, kb_ref[...])
        e = jnp.exp(s - jnp.max(s, axis=-1, keepdims=True))
        return e / jnp.sum(e, axis=-1, keepdims=True)

    def body(i, carry):
        rows = pl.ds(pl.multiple_of(i * qblk, qblk), qblk)
        q = qb_ref[rows, :]
        zero = jnp.zeros_like(q)
        a = softmax_rows(jnp.where(first, q, zero)) - lam_full * softmax_rows(jnp.where(first, zero, q))
        o = _dot(a.astype(BF16), vb_ref[...])
        o_ref[rows, :] = (_rms(o, sg_ref[...]) * (1.0 - lam_init)).astype(o_ref.dtype)
        return carry

    lax.fori_loop(0, T // qblk, body, 0)


def _diff(z, tables, lam, subln_g, lam_init, cols):
    B, T, _ = z.shape
    q0, k0, v0 = cols
    W = 2 * DIFF_DH
    zspec = lambda off: pl.BlockSpec((None, T, W), lambda b, h: (b, 0, off // W + h))
    tspec = pl.BlockSpec((T, LANES), lambda b, h: (0, 0))
    return pl.pallas_call(
        functools.partial(_diff_kernel, lam_init=lam_init),
        grid=(B, DIFF_HEADS),
        in_specs=[
            zspec(q0), zspec(k0), zspec(v0), tspec, tspec, tspec,
            pl.BlockSpec(lam.shape, lambda b, h: (0, 0)),
            pl.BlockSpec((1, W), lambda b, h: (0, 0)),
        ],
        out_specs=pl.BlockSpec((None, T, W), lambda b, h: (b, 0, h)),
        out_shape=jax.ShapeDtypeStruct((B, T, DIFF_HEADS * W), BF16),
        scratch_shapes=[pltpu.VMEM((T, W), BF16), pltpu.VMEM((T, W), BF16), pltpu.VMEM((T, W), BF16)],
        compiler_params=_params("parallel", "arbitrary"),
        name="diff_attn",
    )(z, z, z, *tables, lam, subln_g.reshape(1, W))


_EVEN_MAIN = 2 * GLA_HEADS * GLA_DK + 2 * GLA_HEADS * GLA_DV
_EVEN_NA = 3 * NA_HEADS * HEAD_DIM
_EVEN_COLS_GLA = (0, GLA_HEADS * GLA_DK, 2 * GLA_HEADS * GLA_DK, 2 * GLA_HEADS * GLA_DK + GLA_HEADS * GLA_DV,
                  _EVEN_MAIN + _EVEN_NA)
_EVEN_COLS_NA = tuple(_EVEN_MAIN + i * NA_HEADS * HEAD_DIM for i in range(3))
_ODD_Q = len(DIL_PAIRS) * DIL_HEADS * HEAD_DIM
_ODD_COLS_DIL = (0, _ODD_Q, _ODD_Q + DIL_HEADS * HEAD_DIM)
_ODD_COLS_DIFF = tuple(_ODD_Q + 2 * DIL_HEADS * HEAD_DIM + i * DIFF_HEADS * 2 * DIFF_DH for i in range(3))


def _prepare_even(w_in, w_out, wg_f, bg_f, wg_b, bg_b, rpb):
    D = w_in.shape[0]
    lr0 = _EVEN_MAIN
    lr1 = lr0 + 2 * GLA_RANK
    w = jnp.concatenate([w_in[:, :lr0], w_in[:, lr1:], w_in[:, lr0:lr1],
                         jnp.zeros((D, LANES - 2 * GLA_RANK), w_in.dtype)], axis=1).astype(BF16)
    pad = lambda a, before: jnp.pad(a, ((before, LANES - GLA_RANK - before), (0, 0))).astype(BF16)
    ya_width = GLA_HEADS * GLA_DV
    return dict(w_in=w, wa=w_out[:ya_width].astype(BF16), wb=w_out[ya_width:].astype(BF16),
                wgf=pad(wg_f, 0), wgb=pad(wg_b, GLA_RANK),
                bgf=bg_f.reshape(1, -1).astype(F32), bgb=bg_b.reshape(1, -1).astype(F32),
                bias=_na_bias_table(rpb))


def _trunk(x, p, w):
    B, T, D = x.shape
    M = B * T
    h = x.reshape(M, D)
    for i in range(DEPTH):
        g = w["norm_g"][i]
        j = i // 2
        if i % 2 == 0:
            e = w["even"][j]
            z = _norm_matmul(h, g[0], e["w_in"], F32).reshape(B, T, -1)
            ya = _gla(z, e["wgf"], e["bgf"], e["wgb"], e["bgb"], w["gla_norm_g"][j].reshape(1, -1), _EVEN_COLS_GLA)
            yb = _na(z, e["bias"], _EVEN_COLS_NA)
            wa, wb = e["wa"], e["wb"]
        else:
            o = w["odd"][j]
            z = _norm_matmul(h, g[0], o["w_in"], F32).reshape(B, T, -1)
            ya = _dilated(z, w["rope_dil"], _ODD_COLS_DIL)
            yb = _diff(z, w["rope_diff"], w["diff_lambda"][j], w["diff_subln_g"][j],
                       0.8 - 0.6 * math.exp(-0.3 * i), _ODD_COLS_DIFF)
            wa, wb = o["wa"], o["wb"]
        h = _out_proj(h, ya.reshape(M, -1), yb.reshape(M, -1), g[1], wa, wb)
        h = _ffn(h, g[2], g[3], w["ffn_gate"][i], w["ffn_up"][i], w["ffn_down"][i])
        h = _ple(h, p[i].reshape(M, -1), g[4], w["ple_proj"][i], w["ple_gate"][i])
    return h.reshape(B, T, D)


def kernel(x_prompt, x_sample, p_prompt, p_sample, norm_g, w_in_even, w_out_even, gla_wg_fwd, gla_bg_fwd, gla_wg_bwd, gla_bg_bwd, gla_norm_g, na_rpb, w_in_odd, w_out_odd, diff_lambda, diff_subln_g, w_ffn_gate, w_ffn_up, w_ffn_down, w_ple_proj, w_ple_gate):
    T = x_prompt.shape[1]
    assert x_sample.shape[1] == T
    yc_width = DIL_HEADS * HEAD_DIM
    w = dict(
        norm_g=norm_g.astype(F32),
        even=[_prepare_even(w_in_even[j], w_out_even[j], gla_wg_fwd[j], gla_bg_fwd[j], gla_wg_bwd[j],
                            gla_bg_bwd[j], na_rpb[j]) for j in range(w_in_even.shape[0])],
        odd=[dict(w_in=w_in_odd[j].astype(BF16), wa=w_out_odd[j, :yc_width].astype(BF16),
                  wb=w_out_odd[j, yc_width:].astype(BF16)) for j in range(w_in_odd.shape[0])],
        gla_norm_g=gla_norm_g.astype(F32),
        diff_lambda=diff_lambda.astype(F32),
        diff_subln_g=diff_subln_g.astype(F32),
        rope_dil=_rope_tables(T, ROT_DIM, HEAD_DIM),
        rope_diff=_rope_tables(T, DIFF_DH // 4, DIFF_DH),
        ffn_gate=w_ffn_gate.astype(BF16), ffn_up=w_ffn_up.astype(BF16), ffn_down=w_ffn_down.astype(BF16),
        ple_proj=w_ple_proj.astype(BF16), ple_gate=w_ple_gate.astype(BF16),
    )
    return (_trunk(x_prompt, p_prompt, w), _trunk(x_sample, p_sample, w))
```

```python
import functools
import math

import jax
import jax.numpy as jnp
from jax import lax
from jax.experimental import pallas as pl
from jax.experimental.pallas import tpu as pltpu

F32 = jnp.float32
BF16 = jnp.bfloat16

D_MODEL = 2048
DEPTH = 2
HEAD_DIM = 128
PLE_DIM = 256
GRID_W = 64
EPS = 1e-6
NEG = -1e30
ROPE_THETA = 500000.0
ROT_DIM = HEAD_DIM // 4
D_FF = 5632
GLA_HEADS = 4
GLA_DV = 256
GLA_DK = 128
GLA_RANK = 16
GLA_TAU = 16.0
GLA_CHUNK = 64
NA_HEADS = 8
NA_WIN_ROWS = 8
NA_WIN_COLS = 16
DIL_PAIRS = ((128, 1), (512, 4), (2048, 16))
DIL_HEADS = 8
DIFF_DH = 64
DIFF_HEADS = 8

LANES = 128
VMEM_LIMIT = 56 * 1024 * 1024

TOKEN_TILE = 512
FFN_TILE = 512
IN_PROJ_TILE = 1024
ATTN_Q_BLOCK = 256
DIL_Q_BLOCK = 512
NA_GROUP = 4
NA_GROUP_WIN = NA_WIN_ROWS + NA_GROUP - 1

NT_DIMS = (((1,), (1,)), ((), ()))
TN_DIMS = (((0,), (0,)), ((), ()))


def _params(*semantics):
    return pltpu.CompilerParams(dimension_semantics=semantics, vmem_limit_bytes=VMEM_LIMIT)


def _divisor_tile(n, cap):
    return max(t for t in range(LANES, cap + 1, LANES) if n % t == 0)


def _rms(x, g):
    return x * lax.rsqrt(jnp.mean(x * x, axis=-1, keepdims=True) + EPS) * g


def _sigmoid(x):
    return 1.0 / (1.0 + jnp.exp(-x))


def _dot(a, b):
    return jnp.dot(a, b, preferred_element_type=F32)


def _dot_nt(a, b):
    return lax.dot_general(a, b, NT_DIMS, preferred_element_type=F32)


def _norm_matmul_kernel(x_ref, g_ref, w_ref, o_ref, a_ref):
    @pl.when(pl.program_id(1) == 0)
    def _():
        a_ref[...] = _rms(x_ref[...], g_ref[...]).astype(BF16)

    o_ref[...] = _dot(a_ref[...], w_ref[...]).astype(o_ref.dtype)


def _norm_matmul(x, g, w, out_dtype):
    M, D = x.shape
    N = w.shape[1]
    tm, tn = TOKEN_TILE, _divisor_tile(N, IN_PROJ_TILE)
    assert M % tm == 0
    return pl.pallas_call(
        _norm_matmul_kernel,
        grid=(M // tm, N // tn),
        in_specs=[
            pl.BlockSpec((tm, D), lambda i, j: (i, 0)),
            pl.BlockSpec((1, D), lambda i, j: (0, 0)),
            pl.BlockSpec((D, tn), lambda i, j: (0, j)),
        ],
        out_specs=pl.BlockSpec((tm, tn), lambda i, j: (i, j)),
        out_shape=jax.ShapeDtypeStruct((M, N), out_dtype),
        scratch_shapes=[pltpu.VMEM((tm, D), BF16)],
        compiler_params=_params("parallel", "arbitrary"),
        name="norm_in_proj",
    )(x, g.reshape(1, D), w)


def _out_proj_kernel(h_ref, ya_ref, yb_ref, g_ref, wa_ref, wb_ref, o_ref):
    m = _dot(ya_ref[...], wa_ref[...]) + _dot(yb_ref[...], wb_ref[...])
    o_ref[...] = h_ref[...] + _rms(m, g_ref[...])


def _out_proj(h, ya, yb, g, wa, wb):
    M, D = h.shape
    Ka, Kb = ya.shape[1], yb.shape[1]
    tm = TOKEN_TILE
    return pl.pallas_call(
        _out_proj_kernel,
        grid=(M // tm,),
        in_specs=[
            pl.BlockSpec((tm, D), lambda i: (i, 0)),
            pl.BlockSpec((tm, Ka), lambda i: (i, 0)),
            pl.BlockSpec((tm, Kb), lambda i: (i, 0)),
            pl.BlockSpec((1, D), lambda i: (0, 0)),
            pl.BlockSpec((Ka, D), lambda i: (0, 0)),
            pl.BlockSpec((Kb, D), lambda i: (0, 0)),
        ],
        out_specs=pl.BlockSpec((tm, D), lambda i: (i, 0)),
        out_shape=jax.ShapeDtypeStruct((M, D), F32),
        compiler_params=_params("parallel"),
        name="out_proj",
    )(h, ya, yb, g.reshape(1, D), wa, wb)


def _ffn_kernel(h_ref, g2_ref, g3_ref, wg_ref, wu_ref, wd_ref, o_ref, a_ref, acc_ref):
    f = pl.program_id(1)

    @pl.when(f == 0)
    def _():
        a_ref[...] = _rms(h_ref[...], g2_ref[...]).astype(BF16)
        acc_ref[...] = jnp.zeros_like(acc_ref)

    a = a_ref[...]
    gate = _dot(a, wg_ref[...])
    up = _dot(a, wu_ref[...])
    act = (gate * _sigmoid(gate) * up).astype(BF16)
    acc_ref[...] += _dot(act, wd_ref[...])

    @pl.when(f == pl.num_programs(1) - 1)
    def _():
        o_ref[...] = h_ref[...] + _rms(acc_ref[...], g3_ref[...])


def _ffn(h, g2, g3, wg, wu, wd):
    M, D = h.shape
    F = wg.shape[1]
    tm, tf = TOKEN_TILE, FFN_TILE
    assert M % tm == 0 and F % tf == 0
    return pl.pallas_call(
        _ffn_kernel,
        grid=(M // tm, F // tf),
        in_specs=[
            pl.BlockSpec((tm, D), lambda i, f: (i, 0)),
            pl.BlockSpec((1, D), lambda i, f: (0, 0)),
            pl.BlockSpec((1, D), lambda i, f: (0, 0)),
            pl.BlockSpec((D, tf), lambda i, f: (0, f)),
            pl.BlockSpec((D, tf), lambda i, f: (0, f)),
            pl.BlockSpec((tf, D), lambda i, f: (f, 0)),
        ],
        out_specs=pl.BlockSpec((tm, D), lambda i, f: (i, 0)),
        out_shape=jax.ShapeDtypeStruct((M, D), F32),
        scratch_shapes=[pltpu.VMEM((tm, D), BF16), pltpu.VMEM((tm, D), F32)],
        compiler_params=_params("parallel", "arbitrary"),
        name="ffn",
    )(h, g2.reshape(1, D), g3.reshape(1, D), wg, wu, wd)


def _ple_kernel(h_ref, p_ref, g_ref, wp_ref, wgate_ref, o_ref):
    h = h_ref[...]
    e = _rms(_dot(p_ref[...].astype(BF16), wp_ref[...]), g_ref[...])
    gate = _sigmoid(_dot(h.astype(BF16), wgate_ref[...]))
    o_ref[...] = h + e * gate


def _ple(h, p, g, wp, wgate):
    M, D = h.shape
    P = p.shape[1]
    tm = TOKEN_TILE
    return pl.pallas_call(
        _ple_kernel,
        grid=(M // tm,),
        in_specs=[
            pl.BlockSpec((tm, D), lambda i: (i, 0)),
            pl.BlockSpec((tm, P), lambda i: (i, 0)),
            pl.BlockSpec((1, D), lambda i: (0, 0)),
            pl.BlockSpec((P, D), lambda i: (0, 0)),
            pl.BlockSpec((D, D), lambda i: (0, 0)),
        ],
        out_specs=pl.BlockSpec((tm, D), lambda i: (i, 0)),
        out_shape=jax.ShapeDtypeStruct((M, D), F32),
        compiler_params=_params("parallel"),
        name="ple",
    )(h, p, g.reshape(1, D), wp, wgate)


def _cumsum_rows(tri, g):
    g1 = g.astype(BF16)
    r1 = g - g1.astype(F32)
    g2 = r1.astype(BF16)
    g3 = (r1 - g2.astype(F32)).astype(BF16)
    return _dot(tri, g1) + _dot(tri, g2) + _dot(tri, g3)


def _log_sigmoid(x):
    return jnp.minimum(x, 0.0) - jnp.log1p(jnp.exp(-jnp.abs(x)))


def _gla_kernel(q_ref, k_ref, v_ref, r_ref, lr_ref, wgf_ref, bgf_ref, wgb_ref, bgb_ref, gn_ref, o_ref,
                gf_ref, gb_ref, of_ref, ob_ref, sf_ref, sb_ref):
    T = q_ref.shape[0]
    L = GLA_CHUNK
    N = T // L
    lr = lr_ref[...].astype(BF16)
    gf_ref[...] = _log_sigmoid(_dot(lr, wgf_ref[...]) + bgf_ref[...]) / GLA_TAU
    gb_ref[...] = _log_sigmoid(_dot(lr, wgb_ref[...]) + bgb_ref[...]) / GLA_TAU
    sf_ref[...] = jnp.zeros_like(sf_ref)
    sb_ref[...] = jnp.zeros_like(sb_ref)

    row = lax.broadcasted_iota(jnp.int32, (L, L), 0)
    col = lax.broadcasted_iota(jnp.int32, (L, L), 1)
    scale = GLA_DK ** -0.5

    def one_direction(n, g_ref, cum_mask, att_mask, last_row, s_ref, out_ref):
        rows = pl.ds(pl.multiple_of(n * L, L), L)
        b = _cumsum_rows(cum_mask.astype(BF16), g_ref[rows, :])
        k = k_ref[rows, :]
        v = v_ref[rows, :].astype(BF16)
        qd = (q_ref[rows, :] * scale * jnp.exp(b)).astype(BF16)
        kd = (k * jnp.exp(-b)).astype(BF16)
        att = jnp.where(att_mask, _dot_nt(qd, kd), 0.0).astype(BF16)
        state = s_ref[...]
        out_ref[rows, :] = _dot(att, v) + _dot_nt(qd, state.astype(BF16))
        b_last = b[last_row:last_row + 1, :]
        ku = (k * jnp.exp(b_last - b)).astype(BF16)
        s_ref[...] = state * jnp.exp(b_last) + lax.dot_general(v, ku, TN_DIMS, preferred_element_type=F32)

    def body(n, carry):
        one_direction(n, gf_ref, col <= row, col <= row, L - 1, sf_ref, of_ref)
        one_direction(N - 1 - n, gb_ref, col >= row, col > row, 0, sb_ref, ob_ref)
        return carry

    lax.fori_loop(0, N, body, 0)

    r = r_ref[...]
    o = _rms(of_ref[...] + ob_ref[...], gn_ref[...])
    o_ref[...] = (o * (r * _sigmoid(r))).astype(o_ref.dtype)


def _gla(z, wgf, bgf, wgb, bgb, gn, cols):
    B, T, _ = z.shape
    dk, dv = GLA_DK, GLA_DV
    q0, k0, v0, r0, lr0 = cols
    zspec = lambda width, off: pl.BlockSpec((None, T, width), lambda b, h: (b, 0, off // width + h))
    return pl.pallas_call(
        _gla_kernel,
        grid=(B, GLA_HEADS),
        in_specs=[
            zspec(dk, q0), zspec(dk, k0), zspec(dv, v0), zspec(dv, r0),
            pl.BlockSpec((None, T, LANES), lambda b, h: (b, 0, lr0 // LANES)),
            pl.BlockSpec((LANES, dk), lambda b, h: (0, h)),
            pl.BlockSpec((1, dk), lambda b, h: (0, h)),
            pl.BlockSpec((LANES, dk), lambda b, h: (0, h)),
            pl.BlockSpec((1, dk), lambda b, h: (0, h)),
            pl.BlockSpec((1, dv), lambda b, h: (0, 0)),
        ],
        out_specs=pl.BlockSpec((None, T, dv), lambda b, h: (b, 0, h)),
        out_shape=jax.ShapeDtypeStruct((B, T, GLA_HEADS * dv), BF16),
        scratch_shapes=[
            pltpu.VMEM((T, dk), F32), pltpu.VMEM((T, dk), F32),
            pltpu.VMEM((T, dv), F32), pltpu.VMEM((T, dv), F32),
            pltpu.VMEM((dv, dk), F32), pltpu.VMEM((dv, dk), F32),
        ],
        compiler_params=_params("parallel", "arbitrary"),
        name="gla",
    )(z, z, z, z, z, wgf, bgf, wgb, bgb, gn)


def _na_window_start(gi, rows):
    lo, hi = 0, rows - NA_GROUP_WIN
    start = NA_GROUP * gi - NA_WIN_ROWS // 2
    return min(max(start, lo), hi) if isinstance(gi, int) else jnp.clip(start, lo, hi)


def _na_kernel(q_ref, k_ref, v_ref, bias_ref, o_ref, kb_ref, vx_ref):
    T = q_ref.shape[0]
    rows = T // GRID_W
    groups = rows // NA_GROUP
    qlen = NA_GROUP * GRID_W
    klen = NA_GROUP_WIN * GRID_W
    kb_ref[...] = k_ref[...].astype(BF16)
    vx_ref[:, :HEAD_DIM] = v_ref[...].astype(BF16)
    vx_ref[:, HEAD_DIM:] = jnp.ones((T, HEAD_DIM), BF16)
    scale = HEAD_DIM ** -0.5

    def body(gi, carry):
        variant = jnp.where(gi == 0, 0, jnp.where(gi == groups - 1, 2, 1))
        qrows = pl.ds(pl.multiple_of(gi * qlen, qlen), qlen)
        krows = pl.ds(pl.multiple_of(_na_window_start(gi, rows) * GRID_W, GRID_W), klen)
        s = _dot_nt(q_ref[qrows, :].astype(BF16), kb_ref[krows, :]) * scale + bias_ref[variant]
        e = jnp.exp(s - jnp.max(s, axis=-1, keepdims=True)).astype(BF16)
        ox = _dot(e, vx_ref[krows, :])
        o_ref[qrows, :] = (ox[:, :HEAD_DIM] / ox[:, HEAD_DIM:]).astype(o_ref.dtype)
        return carry

    lax.fori_loop(0, groups, body, 0)


def _na(z, bias, cols):
    B, T, _ = z.shape
    q0, k0, v0 = cols
    zspec = lambda off: pl.BlockSpec((None, T, HEAD_DIM), lambda b, h: (b, 0, off // HEAD_DIM + h))
    return pl.pallas_call(
        _na_kernel,
        grid=(B, NA_HEADS),
        in_specs=[
            zspec(q0), zspec(k0), zspec(v0),
            pl.BlockSpec((None,) + bias.shape[1:], lambda b, h: (h, 0, 0, 0)),
        ],
        out_specs=pl.BlockSpec((None, T, HEAD_DIM), lambda b, h: (b, 0, h)),
        out_shape=jax.ShapeDtypeStruct((B, T, NA_HEADS * HEAD_DIM), BF16),
        scratch_shapes=[pltpu.VMEM((T, HEAD_DIM), BF16), pltpu.VMEM((T, 2 * HEAD_DIM), BF16)],
        compiler_params=_params("parallel", "arbitrary"),
        name="neighbourhood_attn",
    )(z, z, z, bias)


def _na_bias_table(rpb, T):
    H = rpb.shape[0]
    W = GRID_W
    rows = T // W
    groups = rows // NA_GROUP
    assert rows % NA_GROUP == 0 and rows >= NA_GROUP_WIN and groups >= 3
    c = jnp.arange(W)
    c0 = jnp.clip(c - NA_WIN_COLS // 2, 0, W - NA_WIN_COLS)
    col_ok = (c[None, :] >= c0[:, None]) & (c[None, :] < c0[:, None] + NA_WIN_COLS)
    dc = c[None, :] - c[:, None] + NA_WIN_COLS - 1
    onehot = (dc[None] == jnp.arange(2 * NA_WIN_COLS - 1)[:, None, None]).astype(F32)
    by_dr = jnp.einsum("hrj,jqk->hrqk", rpb.astype(F32), onehot, precision=lax.Precision.HIGHEST)
    by_dr = jnp.where(col_ok, by_dr, NEG)
    masked = jnp.full((H, W, W), NEG, F32)
    variants = []
    for gi in (0, 1, groups - 1):
        ws = _na_window_start(gi, rows)
        per_row = []
        for ri in range(NA_GROUP):
            r = NA_GROUP * gi + ri
            kr0 = min(max(r - NA_WIN_ROWS // 2, 0), rows - NA_WIN_ROWS)
            blocks = []
            for o in range(NA_GROUP_WIN):
                key_row = ws + o
                ok = kr0 <= key_row < kr0 + NA_WIN_ROWS
                blocks.append(by_dr[:, key_row - r + NA_WIN_ROWS - 1] if ok else masked)
            per_row.append(jnp.concatenate(blocks, axis=-1))
        variants.append(jnp.concatenate(per_row, axis=1))
    return jnp.stack(variants, axis=1)


def _rope_tables(T, rot, period):
    half = rot // 2
    inv = ROPE_THETA ** (-jnp.arange(half, dtype=F32) / half)
    ang = jnp.arange(T, dtype=F32)[:, None] * inv[None, :]
    cos, sin = jnp.cos(ang), jnp.sin(ang)
    zeros = jnp.zeros((T, half), F32)
    rest0 = jnp.zeros((T, period - rot), F32)
    c = jnp.concatenate([cos, cos, jnp.ones((T, period - rot), F32)], axis=1)
    sa = jnp.concatenate([zeros, sin, rest0], axis=1)
    sb = jnp.concatenate([-sin, zeros, rest0], axis=1)
    rep = LANES // period
    return tuple(jnp.tile(t, (1, rep)) for t in (c, sa, sb))


def _rope(x, c, sa, sb, half):
    return x * c + pltpu.roll(x, half, 1) * sa + pltpu.roll(x, LANES - half, 1) * sb


def _class_rows(x, n, dil):
    return pl.ds(x, n) if dil == 1 else pl.ds(x, n, stride=dil)


def _dil_block_plan(T):
    qb = DIL_Q_BLOCK
    plan = []
    for win, dil in DIL_PAIRS:
        radius = win // (2 * dil)
        n = T // dil
        if n > qb:
            kw = qb + 2 * radius
            blocks = []
            for q_start in range(0, T, qb):
                seg = (q_start // n) * n
                k_start = min(max(q_start - radius, seg), seg + n - kw)
                blocks.append((q_start, k_start, k_start - q_start))
        else:
            kw = qb
            blocks = [(q_start, q_start, 0) for q_start in range(0, T, qb)]
        plan.append(dict(dil=dil, n=n, radius=radius, kw=kw, blocks=blocks))
    return plan


def _dil_masks(T):
    qb = DIL_Q_BLOCK
    tables, index = [], {}
    for g, grp in enumerate(_dil_block_plan(T)):
        r = jnp.arange(qb)[:, None]
        c = jnp.arange(grp["kw"])[None, :]
        for _, _, off in grp["blocks"]:
            if (g, off) in index:
                continue
            ok = jnp.abs(c + off - r) <= grp["radius"]
            if grp["n"] < qb:
                ok = ok & (c // grp["n"] == r // grp["n"])
            index[(g, off)] = len(tables)
            tables.append(jnp.where(ok, 0.0, NEG).astype(F32))
    return tables, index


def _dil_kernel(*refs, T, mask_index):
    n_masks = len(set(mask_index.values()))
    q_refs = refs[:3]
    k_ref, v_ref, c_ref, sa_ref, sb_ref = refs[3:8]
    mask_refs = refs[8:8 + n_masks]
    o_ref, kr_ref, qr_ref, on_ref, ln_ref = refs[8 + n_masks:]
    half = ROT_DIM // 2
    qb = DIL_Q_BLOCK
    rope = lambda x: _rope(x, c_ref[...], sa_ref[...], sb_ref[...], half)
    kr_ref[...] = rope(k_ref[...])
    scale = HEAD_DIM ** -0.5

    def class_major_pieces(start, length, grp):
        n, dil = grp["n"], grp["dil"]
        pos = start
        while pos < start + length:
            x, i = divmod(pos, n)
            take = min(n - i, start + length - pos)
            yield _class_rows(x + i * dil, take, dil), pos - start, take
            pos += take

    def gather(ref, start, length, grp):
        pieces = [ref[rows, :].astype(BF16) for rows, _, _ in class_major_pieces(start, length, grp)]
        return pieces[0] if len(pieces) == 1 else jnp.concatenate(pieces, axis=0)

    def scatter(ref, g, start, value, grp):
        for rows, off, take in class_major_pieces(start, value.shape[0], grp):
            ref[g, rows, :] = value[off:off + take]

    for g, grp in enumerate(_dil_block_plan(T)):
        qr_ref[...] = rope(q_refs[g][...]) * scale
        for q_start, k_start, off in grp["blocks"]:
            kw = grp["kw"]
            q = gather(qr_ref, q_start, qb, grp)
            k = gather(kr_ref, k_start, kw, grp)
            vx = jnp.concatenate([gather(v_ref, k_start, kw, grp), jnp.ones((kw, HEAD_DIM), BF16)], axis=1)
            s = _dot_nt(q, k) + mask_refs[mask_index[(g, off)]][...]
            m = jnp.max(s, axis=-1, keepdims=True)
            ox = _dot(jnp.exp(s - m).astype(BF16), vx)
            l = ox[:, HEAD_DIM:]
            scatter(on_ref, g, q_start, ox[:, :HEAD_DIM] / l, grp)
            scatter(ln_ref, g, q_start, m + jnp.log(l), grp)

    l0, l1, l2 = ln_ref[0], ln_ref[1], ln_ref[2]
    m = jnp.maximum(jnp.maximum(l0, l1), l2)
    e0, e1, e2 = jnp.exp(l0 - m), jnp.exp(l1 - m), jnp.exp(l2 - m)
    den = e0 + e1 + e2
    o = (e0 / den) * on_ref[0] + (e1 / den) * on_ref[1] + (e2 / den) * on_ref[2]
    o_ref[...] = o.astype(o_ref.dtype)


def _dilated(z, tables, masks, mask_index, cols):
    B, T, _ = z.shape
    q0, k0, v0 = cols
    G = len(DIL_PAIRS)
    assert G == 3 and all(T % (dil * GLA_CHUNK) == 0 for _, dil in DIL_PAIRS) and T % DIL_Q_BLOCK == 0
    zspec = lambda off: pl.BlockSpec((None, T, HEAD_DIM), lambda b, h: (b, 0, off // HEAD_DIM + h))
    const = lambda a: pl.BlockSpec(a.shape, lambda b, h: (0,) * a.ndim)
    return pl.pallas_call(
        functools.partial(_dil_kernel, T=T, mask_index=mask_index),
        grid=(B, DIL_HEADS),
        in_specs=([zspec(q0 + g * DIL_HEADS * HEAD_DIM) for g in range(G)] + [zspec(k0), zspec(v0)]
                  + [const(t) for t in tables] + [const(m) for m in masks]),
        out_specs=pl.BlockSpec((None, T, HEAD_DIM), lambda b, h: (b, 0, h)),
        out_shape=jax.ShapeDtypeStruct((B, T, DIL_HEADS * HEAD_DIM), BF16),
        scratch_shapes=[
            pltpu.VMEM((T, HEAD_DIM), F32), pltpu.VMEM((T, HEAD_DIM), F32),
            pltpu.VMEM((G, T, HEAD_DIM), F32), pltpu.VMEM((G, T, LANES), F32),
        ],
        compiler_params=_params("parallel", "arbitrary"),
        name="dilated_attn",
    )(z, z, z, z, z, *tables, *masks)


def _diff_kernel(q_ref, k_ref, v_ref, c_ref, sa_ref, sb_ref, lam_ref, sg_ref, o_ref, qb_ref, kb_ref, vb_ref,
                 *, lam_init):
    T = q_ref.shape[0]
    half = DIFF_DH // 8
    rope = lambda x: _rope(x, c_ref[...], sa_ref[...], sb_ref[...], half)
    qb_ref[...] = (rope(q_ref[...]) * DIFF_DH ** -0.5).astype(BF16)
    kb_ref[...] = rope(k_ref[...]).astype(BF16)
    vb_ref[...] = v_ref[...].astype(BF16)
    lam = lam_ref[...]
    lam_full = (jnp.exp(jnp.sum(lam[0:1] * lam[1:2], axis=-1, keepdims=True))
                - jnp.exp(jnp.sum(lam[2:3] * lam[3:4], axis=-1, keepdims=True)) + lam_init)
    first = lax.broadcasted_iota(jnp.int32, (1, LANES), 1) < DIFF_DH
    qblk = ATTN_Q_BLOCK

    def exp_rows(q):
        s = _dot_nt(q, kb_ref[...])
        e = jnp.exp(s - jnp.max(s, axis=-1, keepdims=True))
        return e, jnp.sum(e, axis=-1, keepdims=True)

    def body(i, carry):
        rows = pl.ds(pl.multiple_of(i * qblk, qblk), qblk)
        q = qb_ref[rows, :]
        zero = jnp.zeros_like(q)
        e0, l0 = exp_rows(jnp.where(first, q, zero))
        e1, l1 = exp_rows(jnp.where(first, zero, q))
        a = e0 * (1.0 / l0) - e1 * (lam_full / l1)
        o = _dot(a.astype(BF16), vb_ref[...])
        o_ref[rows, :] = (_rms(o, sg_ref[...]) * (1.0 - lam_init)).astype(o_ref.dtype)
        return carry

    lax.fori_loop(0, T // qblk, body, 0)


def _diff(z, tables, lam, subln_g, lam_init, cols):
    B, T, _ = z.shape
    q0, k0, v0 = cols
    W = 2 * DIFF_DH
    zspec = lambda off: pl.BlockSpec((None, T, W), lambda b, h: (b, 0, off // W + h))
    tspec = pl.BlockSpec((T, LANES), lambda b, h: (0, 0))
    return pl.pallas_call(
        functools.partial(_diff_kernel, lam_init=lam_init),
        grid=(B, DIFF_HEADS),
        in_specs=[
            zspec(q0), zspec(k0), zspec(v0), tspec, tspec, tspec,
            pl.BlockSpec(lam.shape, lambda b, h: (0, 0)),
            pl.BlockSpec((1, W), lambda b, h: (0, 0)),
        ],
        out_specs=pl.BlockSpec((None, T, W), lambda b, h: (b, 0, h)),
        out_shape=jax.ShapeDtypeStruct((B, T, DIFF_HEADS * W), BF16),
        scratch_shapes=[pltpu.VMEM((T, W), BF16), pltpu.VMEM((T, W), BF16), pltpu.VMEM((T, W), BF16)],
        compiler_params=_params("parallel", "arbitrary"),
        name="diff_attn",
    )(z, z, z, *tables, lam, subln_g.reshape(1, W))


_EVEN_MAIN = 2 * GLA_HEADS * GLA_DK + 2 * GLA_HEADS * GLA_DV
_EVEN_NA = 3 * NA_HEADS * HEAD_DIM
_EVEN_COLS_GLA = (0, GLA_HEADS * GLA_DK, 2 * GLA_HEADS * GLA_DK, 2 * GLA_HEADS * GLA_DK + GLA_HEADS * GLA_DV,
                  _EVEN_MAIN + _EVEN_NA)
_EVEN_COLS_NA = tuple(_EVEN_MAIN + i * NA_HEADS * HEAD_DIM for i in range(3))
_ODD_Q = len(DIL_PAIRS) * DIL_HEADS * HEAD_DIM
_ODD_COLS_DIL = (0, _ODD_Q, _ODD_Q + DIL_HEADS * HEAD_DIM)
_ODD_COLS_DIFF = tuple(_ODD_Q + 2 * DIL_HEADS * HEAD_DIM + i * DIFF_HEADS * 2 * DIFF_DH for i in range(3))


def _prepare_even(w_in, w_out, wg_f, bg_f, wg_b, bg_b, rpb, T):
    D = w_in.shape[0]
    lr0 = _EVEN_MAIN
    lr1 = lr0 + 2 * GLA_RANK
    w = jnp.concatenate([w_in[:, :lr0], w_in[:, lr1:], w_in[:, lr0:lr1],
                         jnp.zeros((D, LANES - 2 * GLA_RANK), w_in.dtype)], axis=1).astype(BF16)
    pad = lambda a, before: jnp.pad(a, ((before, LANES - GLA_RANK - before), (0, 0))).astype(BF16)
    ya_width = GLA_HEADS * GLA_DV
    return dict(w_in=w, wa=w_out[:ya_width].astype(BF16), wb=w_out[ya_width:].astype(BF16),
                wgf=pad(wg_f, 0), wgb=pad(wg_b, GLA_RANK),
                bgf=bg_f.reshape(1, -1).astype(F32), bgb=bg_b.reshape(1, -1).astype(F32),
                bias=_na_bias_table(rpb, T))


def _trunk(x, p, w):
    B, T, D = x.shape
    M = B * T
    h = x.reshape(M, D)
    for i in range(DEPTH):
        g = w["norm_g"][i]
        j = i // 2
        if i % 2 == 0:
            e = w["even"][j]
            z = _norm_matmul(h, g[0], e["w_in"], F32).reshape(B, T, -1)
            ya = _gla(z, e["wgf"], e["bgf"], e["wgb"], e["bgb"], w["gla_norm_g"][j].reshape(1, -1), _EVEN_COLS_GLA)
            yb = _na(z, e["bias"], _EVEN_COLS_NA)
            wa, wb = e["wa"], e["wb"]
        else:
            o = w["odd"][j]
            z = _norm_matmul(h, g[0], o["w_in"], F32).reshape(B, T, -1)
            ya = _dilated(z, w["rope_dil"], *w["dil_masks"], _ODD_COLS_DIL)
            yb = _diff(z, w["rope_diff"], w["diff_lambda"][j], w["diff_subln_g"][j],
                       0.8 - 0.6 * math.exp(-0.3 * i), _ODD_COLS_DIFF)
            wa, wb = o["wa"], o["wb"]
        h = _out_proj(h, ya.reshape(M, -1), yb.reshape(M, -1), g[1], wa, wb)
        h = _ffn(h, g[2], g[3], w["ffn_gate"][i], w["ffn_up"][i], w["ffn_down"][i])
        h = _ple(h, p[i].reshape(M, -1), g[4], w["ple_proj"][i], w["ple_gate"][i])
    return h.reshape(B, T, D)


def kernel(x_prompt, x_sample, p_prompt, p_sample, norm_g, w_in_even, w_out_even, gla_wg_fwd, gla_bg_fwd, gla_wg_bwd, gla_bg_bwd, gla_norm_g, na_rpb, w_in_odd, w_out_odd, diff_lambda, diff_subln_g, w_ffn_gate, w_ffn_up, w_ffn_down, w_ple_proj, w_ple_gate):
    T = x_prompt.shape[1]
    assert x_sample.shape[1] == T
    yc_width = DIL_HEADS * HEAD_DIM
    w = dict(
        norm_g=norm_g.astype(F32),
        even=[_prepare_even(w_in_even[j], w_out_even[j], gla_wg_fwd[j], gla_bg_fwd[j], gla_wg_bwd[j],
                            gla_bg_bwd[j], na_rpb[j], T) for j in range(w_in_even.shape[0])],
        odd=[dict(w_in=w_in_odd[j].astype(BF16), wa=w_out_odd[j, :yc_width].astype(BF16),
                  wb=w_out_odd[j, yc_width:].astype(BF16)) for j in range(w_in_odd.shape[0])],
        gla_norm_g=gla_norm_g.astype(F32),
        diff_lambda=diff_lambda.astype(F32),
        diff_subln_g=diff_subln_g.astype(F32),
        rope_dil=_rope_tables(T, ROT_DIM, HEAD_DIM),
        dil_masks=_dil_masks(T),
        rope_diff=_rope_tables(T, DIFF_DH // 4, DIFF_DH),
        ffn_gate=w_ffn_gate.astype(BF16), ffn_up=w_ffn_up.astype(BF16), ffn_down=w_ffn_down.astype(BF16),
        ple_proj=w_ple_proj.astype(BF16), ple_gate=w_ple_gate.astype(BF16),
    )
    return (_trunk(x_prompt, p_prompt, w), _trunk(x_sample, p_sample, w))
```

```python
import functools
import math

import jax
import jax.numpy as jnp
from jax import lax
from jax.experimental import pallas as pl
from jax.experimental.pallas import tpu as pltpu

F32 = jnp.float32
BF16 = jnp.bfloat16

D_MODEL = 2048
DEPTH = 2
HEAD_DIM = 128
PLE_DIM = 256
GRID_W = 64
EPS = 1e-6
NEG = -1e30
LOG2E = math.log2(math.e)
LN2 = math.log(2.0)
ROPE_THETA = 500000.0
ROT_DIM = HEAD_DIM // 4
D_FF = 5632
GLA_HEADS = 4
GLA_DV = 256
GLA_DK = 128
GLA_RANK = 16
GLA_TAU = 16.0
GLA_CHUNK = 64
NA_HEADS = 8
NA_WIN_ROWS = 8
NA_WIN_COLS = 16
DIL_PAIRS = ((128, 1), (512, 4), (2048, 16))
DIL_HEADS = 8
DIFF_DH = 64
DIFF_HEADS = 8

LANES = 128
VMEM_LIMIT = 56 * 1024 * 1024

TOKEN_TILE = 512
FFN_TILE = 512
IN_PROJ_TILE = 1024
IN_PROJ_ROWS = 1024
GLA_BLOCK = 256
ATTN_Q_BLOCK = 256
DIL_Q_BLOCK = 512
NA_GROUP = 4
NA_GROUP_WIN = NA_WIN_ROWS + NA_GROUP - 1

NT_DIMS = (((1,), (1,)), ((), ()))
TN_DIMS = (((0,), (0,)), ((), ()))


def _params(*semantics):
    return pltpu.CompilerParams(dimension_semantics=semantics, vmem_limit_bytes=VMEM_LIMIT)


def _divisor_tile(n, cap):
    return max(t for t in range(LANES, cap + 1, LANES) if n % t == 0)


def _rms(x, g):
    return x * lax.rsqrt(jnp.mean(x * x, axis=-1, keepdims=True) + EPS) * g


def _sigmoid(x):
    return 1.0 / (1.0 + jnp.exp(-x))


def _dot(a, b):
    return jnp.dot(a, b, preferred_element_type=F32)


def _dot_nt(a, b):
    return lax.dot_general(a, b, NT_DIMS, preferred_element_type=F32)


def _norm_matmul_kernel(x_ref, g_ref, w_ref, o_ref, a_ref):
    @pl.when(pl.program_id(1) == 0)
    def _():
        a_ref[...] = _rms(x_ref[...], g_ref[...]).astype(BF16)

    o_ref[...] = _dot(a_ref[...], w_ref[...]).astype(o_ref.dtype)


def _norm_matmul(x, g, w, out_dtype):
    M, D = x.shape
    N = w.shape[1]
    tm, tn = IN_PROJ_ROWS, _divisor_tile(N, IN_PROJ_TILE)
    assert M % tm == 0
    return pl.pallas_call(
        _norm_matmul_kernel,
        grid=(M // tm, N // tn),
        in_specs=[
            pl.BlockSpec((tm, D), lambda i, j: (i, 0)),
            pl.BlockSpec((1, D), lambda i, j: (0, 0)),
            pl.BlockSpec((D, tn), lambda i, j: (0, j)),
        ],
        out_specs=pl.BlockSpec((tm, tn), lambda i, j: (i, j)),
        out_shape=jax.ShapeDtypeStruct((M, N), out_dtype),
        scratch_shapes=[pltpu.VMEM((tm, D), BF16)],
        compiler_params=_params("parallel", "arbitrary"),
        name="norm_in_proj",
    )(x, g.reshape(1, D), w)


def _out_proj_kernel(h_ref, ya_ref, yb_ref, g_ref, wa_ref, wb_ref, o_ref):
    m = _dot(ya_ref[...], wa_ref[...]) + _dot(yb_ref[...], wb_ref[...])
    o_ref[...] = h_ref[...] + _rms(m, g_ref[...])


def _out_proj(h, ya, yb, g, wa, wb):
    M, D = h.shape
    Ka, Kb = ya.shape[1], yb.shape[1]
    tm = TOKEN_TILE
    return pl.pallas_call(
        _out_proj_kernel,
        grid=(M // tm,),
        in_specs=[
            pl.BlockSpec((tm, D), lambda i: (i, 0)),
            pl.BlockSpec((tm, Ka), lambda i: (i, 0)),
            pl.BlockSpec((tm, Kb), lambda i: (i, 0)),
            pl.BlockSpec((1, D), lambda i: (0, 0)),
            pl.BlockSpec((Ka, D), lambda i: (0, 0)),
            pl.BlockSpec((Kb, D), lambda i: (0, 0)),
        ],
        out_specs=pl.BlockSpec((tm, D), lambda i: (i, 0)),
        out_shape=jax.ShapeDtypeStruct((M, D), F32),
        compiler_params=_params("parallel"),
        name="out_proj",
    )(h, ya, yb, g.reshape(1, D), wa, wb)


def _ffn_kernel(h_ref, g2_ref, g3_ref, wg_ref, wu_ref, wd_ref, o_ref, a_ref, acc_ref):
    f = pl.program_id(1)

    @pl.when(f == 0)
    def _():
        a_ref[...] = _rms(h_ref[...], g2_ref[...]).astype(BF16)
        acc_ref[...] = jnp.zeros_like(acc_ref)

    a = a_ref[...]
    gate = _dot(a, wg_ref[...])
    up = _dot(a, wu_ref[...])
    act = (gate * _sigmoid(gate) * up).astype(BF16)
    acc_ref[...] += _dot(act, wd_ref[...])

    @pl.when(f == pl.num_programs(1) - 1)
    def _():
        o_ref[...] = h_ref[...] + _rms(acc_ref[...], g3_ref[...])


def _ffn(h, g2, g3, wg, wu, wd):
    M, D = h.shape
    F = wg.shape[1]
    tm, tf = TOKEN_TILE, FFN_TILE
    assert M % tm == 0 and F % tf == 0
    return pl.pallas_call(
        _ffn_kernel,
        grid=(M // tm, F // tf),
        in_specs=[
            pl.BlockSpec((tm, D), lambda i, f: (i, 0)),
            pl.BlockSpec((1, D), lambda i, f: (0, 0)),
            pl.BlockSpec((1, D), lambda i, f: (0, 0)),
            pl.BlockSpec((D, tf), lambda i, f: (0, f)),
            pl.BlockSpec((D, tf), lambda i, f: (0, f)),
            pl.BlockSpec((tf, D), lambda i, f: (f, 0)),
        ],
        out_specs=pl.BlockSpec((tm, D), lambda i, f: (i, 0)),
        out_shape=jax.ShapeDtypeStruct((M, D), F32),
        scratch_shapes=[pltpu.VMEM((tm, D), BF16), pltpu.VMEM((tm, D), F32)],
        compiler_params=_params("parallel", "arbitrary"),
        name="ffn",
    )(h, g2.reshape(1, D), g3.reshape(1, D), wg, wu, wd)


def _ple_kernel(h_ref, p_ref, g_ref, wp_ref, wgate_ref, o_ref):
    h = h_ref[...]
    e = _rms(_dot(p_ref[...].astype(BF16), wp_ref[...]), g_ref[...])
    gate = _sigmoid(_dot(h.astype(BF16), wgate_ref[...]))
    o_ref[...] = h + e * gate


def _ple(h, p, g, wp, wgate):
    M, D = h.shape
    P = p.shape[1]
    tm = TOKEN_TILE
    return pl.pallas_call(
        _ple_kernel,
        grid=(M // tm,),
        in_specs=[
            pl.BlockSpec((tm, D), lambda i: (i, 0)),
            pl.BlockSpec((tm, P), lambda i: (i, 0)),
            pl.BlockSpec((1, D), lambda i: (0, 0)),
            pl.BlockSpec((P, D), lambda i: (0, 0)),
            pl.BlockSpec((D, D), lambda i: (0, 0)),
        ],
        out_specs=pl.BlockSpec((tm, D), lambda i: (i, 0)),
        out_shape=jax.ShapeDtypeStruct((M, D), F32),
        compiler_params=_params("parallel"),
        name="ple",
    )(h, p, g.reshape(1, D), wp, wgate)


def _split3(g):
    g1 = g.astype(BF16)
    r1 = g - g1.astype(F32)
    g2 = r1.astype(BF16)
    g3 = (r1 - g2.astype(F32)).astype(BF16)
    return g1, g2, g3


def _mask_dot(mask, parts):
    m = mask.astype(BF16)
    return _dot(m, parts[0]) + _dot(m, parts[1]) + _dot(m, parts[2])


def _log_sigmoid(x):
    return jnp.minimum(x, 0.0) - jnp.log1p(jnp.exp(-jnp.abs(x)))


def _gla_kernel(q_ref, k_ref, v_ref, r_ref, lr_ref, wgf_ref, bgf_ref, wgb_ref, bgb_ref, gn_ref, o_ref,
                vb_ref, intra_ref, qdf_ref, kuf_ref, totf_ref, interf_ref, sf_ref,
                qdb_ref, kub_ref, totb_ref, interb_ref, sb_ref):
    T = q_ref.shape[0]
    L = GLA_CHUNK
    N = T // L
    R = GLA_BLOCK
    row = lax.broadcasted_iota(jnp.int32, (R, R), 0)
    col = lax.broadcasted_iota(jnp.int32, (R, R), 1)
    same_chunk = (row // L) == (col // L)
    scale = GLA_DK ** -0.5
    vb_ref[...] = v_ref[...].astype(BF16)
    fwd = (wgf_ref, bgf_ref, same_chunk & (col <= row), same_chunk & (col <= row), L - 1, qdf_ref, kuf_ref, totf_ref)
    bwd = (wgb_ref, bgb_ref, same_chunk & (col >= row), same_chunk & (col > row), 0, qdb_ref, kub_ref, totb_ref)

    def local(blk, carry):
        rows = pl.ds(pl.multiple_of(blk * R, R), R)
        lr = lr_ref[rows, :].astype(BF16)
        q = q_ref[rows, :] * scale
        k = k_ref[rows, :]
        v = vb_ref[rows, :]
        intra = None
        for w_ref, b_ref, cum_mask, att_mask, last_row, qd_ref, ku_ref, tot_ref in (fwd, bwd):
            g = _split3(_log_sigmoid(_dot(lr, w_ref[...]) + b_ref[...]) / GLA_TAU)
            b = _mask_dot(cum_mask, g)
            b3 = b.reshape(R // L, L, GLA_DK)
            tot = jnp.broadcast_to(b3[:, last_row:last_row + 1, :], b3.shape).reshape(R, GLA_DK)
            qd = (q * jnp.exp(b)).astype(BF16)
            kd = (k * jnp.exp(-b)).astype(BF16)
            qd_ref[rows, :] = qd
            ku_ref[rows, :] = (k * jnp.exp(tot - b)).astype(BF16)
            tot_ref[rows, :] = tot
            part = _dot(jnp.where(att_mask, _dot_nt(qd, kd), 0.0).astype(BF16), v)
            intra = part if intra is None else intra + part
        intra_ref[rows, :] = intra
        return carry

    lax.fori_loop(0, T // R, local, 0, unroll=2)

    sf_ref[...] = jnp.zeros_like(sf_ref)
    sb_ref[...] = jnp.zeros_like(sb_ref)

    def scan(n, carry):
        for idx, qd_ref, ku_ref, tot_ref, inter_ref, s_ref in (
                (n, qdf_ref, kuf_ref, totf_ref, interf_ref, sf_ref),
                (N - 1 - n, qdb_ref, kub_ref, totb_ref, interb_ref, sb_ref)):
            start = pl.multiple_of(idx * L, L)
            rows = pl.ds(start, L)
            state = s_ref[...]
            inter_ref[rows, :] = _dot_nt(qd_ref[rows, :], state.astype(BF16))
            update = lax.dot_general(vb_ref[rows, :], ku_ref[rows, :], TN_DIMS, preferred_element_type=F32)
            s_ref[...] = state * jnp.exp(tot_ref[pl.ds(start, 1), :]) + update
        return carry

    lax.fori_loop(0, N, scan, 0, unroll=4)

    r = r_ref[...]
    o = _rms(intra_ref[...] + interf_ref[...] + interb_ref[...], gn_ref[...])
    o_ref[...] = (o * (r * _sigmoid(r))).astype(o_ref.dtype)


def _gla(z, wgf, bgf, wgb, bgb, gn, cols):
    B, T, _ = z.shape
    dk, dv = GLA_DK, GLA_DV
    q0, k0, v0, r0, lr0 = cols
    zspec = lambda width, off: pl.BlockSpec((None, T, width), lambda b, h: (b, 0, off // width + h))
    return pl.pallas_call(
        _gla_kernel,
        grid=(B, GLA_HEADS),
        in_specs=[
            zspec(dk, q0), zspec(dk, k0), zspec(dv, v0), zspec(dv, r0),
            pl.BlockSpec((None, T, LANES), lambda b, h: (b, 0, lr0 // LANES)),
            pl.BlockSpec((LANES, dk), lambda b, h: (0, h)),
            pl.BlockSpec((1, dk), lambda b, h: (0, h)),
            pl.BlockSpec((LANES, dk), lambda b, h: (0, h)),
            pl.BlockSpec((1, dk), lambda b, h: (0, h)),
            pl.BlockSpec((1, dv), lambda b, h: (0, 0)),
        ],
        out_specs=pl.BlockSpec((None, T, dv), lambda b, h: (b, 0, h)),
        out_shape=jax.ShapeDtypeStruct((B, T, GLA_HEADS * dv), BF16),
        scratch_shapes=[pltpu.VMEM((T, dv), BF16), pltpu.VMEM((T, dv), F32)] + 2 * [
            pltpu.VMEM((T, dk), BF16), pltpu.VMEM((T, dk), BF16), pltpu.VMEM((T, dk), F32),
            pltpu.VMEM((T, dv), F32), pltpu.VMEM((dv, dk), F32),
        ],
        compiler_params=_params("parallel", "arbitrary"),
        name="gla",
    )(z, z, z, z, z, wgf, bgf, wgb, bgb, gn)


def _na_window_start(gi, rows):
    lo, hi = 0, rows - NA_GROUP_WIN
    start = NA_GROUP * gi - NA_WIN_ROWS // 2
    return min(max(start, lo), hi) if isinstance(gi, int) else jnp.clip(start, lo, hi)


def _na_kernel(q_ref, k_ref, v_ref, bias_ref, o_ref, kb_ref, vx_ref):
    T = q_ref.shape[0]
    rows = T // GRID_W
    groups = rows // NA_GROUP
    qlen = NA_GROUP * GRID_W
    klen = NA_GROUP_WIN * GRID_W
    kb_ref[...] = k_ref[...].astype(BF16)
    vx_ref[:, :HEAD_DIM] = v_ref[...].astype(BF16)
    vx_ref[:, HEAD_DIM:] = jnp.ones((T, HEAD_DIM), BF16)
    scale = HEAD_DIM ** -0.5 * LOG2E

    def body(gi, carry):
        variant = jnp.where(gi == 0, 0, jnp.where(gi == groups - 1, 2, 1))
        qrows = pl.ds(pl.multiple_of(gi * qlen, qlen), qlen)
        krows = pl.ds(pl.multiple_of(_na_window_start(gi, rows) * GRID_W, GRID_W), klen)
        s = _dot_nt(q_ref[qrows, :].astype(BF16), kb_ref[krows, :]) * scale + bias_ref[variant]
        e = jnp.exp2(s - jnp.max(s, axis=-1, keepdims=True)).astype(BF16)
        ox = _dot(e, vx_ref[krows, :])
        o_ref[qrows, :] = (ox[:, :HEAD_DIM] / ox[:, HEAD_DIM:]).astype(o_ref.dtype)
        return carry

    lax.fori_loop(0, groups, body, 0)


def _na(z, bias, cols):
    B, T, _ = z.shape
    q0, k0, v0 = cols
    zspec = lambda off: pl.BlockSpec((None, T, HEAD_DIM), lambda b, h: (b, 0, off // HEAD_DIM + h))
    return pl.pallas_call(
        _na_kernel,
        grid=(B, NA_HEADS),
        in_specs=[
            zspec(q0), zspec(k0), zspec(v0),
            pl.BlockSpec((None,) + bias.shape[1:], lambda b, h: (h, 0, 0, 0)),
        ],
        out_specs=pl.BlockSpec((None, T, HEAD_DIM), lambda b, h: (b, 0, h)),
        out_shape=jax.ShapeDtypeStruct((B, T, NA_HEADS * HEAD_DIM), BF16),
        scratch_shapes=[pltpu.VMEM((T, HEAD_DIM), BF16), pltpu.VMEM((T, 2 * HEAD_DIM), BF16)],
        compiler_params=_params("parallel", "arbitrary"),
        name="neighbourhood_attn",
    )(z, z, z, bias)


def _na_bias_table(rpb, T):
    H = rpb.shape[0]
    W = GRID_W
    rows = T // W
    groups = rows // NA_GROUP
    assert rows % NA_GROUP == 0 and rows >= NA_GROUP_WIN and groups >= 3
    c = jnp.arange(W)
    c0 = jnp.clip(c - NA_WIN_COLS // 2, 0, W - NA_WIN_COLS)
    col_ok = (c[None, :] >= c0[:, None]) & (c[None, :] < c0[:, None] + NA_WIN_COLS)
    dc = c[None, :] - c[:, None] + NA_WIN_COLS - 1
    onehot = (dc[None] == jnp.arange(2 * NA_WIN_COLS - 1)[:, None, None]).astype(F32)
    by_dr = jnp.einsum("hrj,jqk->hrqk", rpb.astype(F32), onehot, precision=lax.Precision.HIGHEST)
    by_dr = jnp.where(col_ok, by_dr * LOG2E, NEG)
    masked = jnp.full((H, W, W), NEG, F32)
    variants = []
    for gi in (0, 1, groups - 1):
        ws = _na_window_start(gi, rows)
        per_row = []
        for ri in range(NA_GROUP):
            r = NA_GROUP * gi + ri
            kr0 = min(max(r - NA_WIN_ROWS // 2, 0), rows - NA_WIN_ROWS)
            blocks = []
            for o in range(NA_GROUP_WIN):
                key_row = ws + o
                ok = kr0 <= key_row < kr0 + NA_WIN_ROWS
                blocks.append(by_dr[:, key_row - r + NA_WIN_ROWS - 1] if ok else masked)
            per_row.append(jnp.concatenate(blocks, axis=-1))
        variants.append(jnp.concatenate(per_row, axis=1))
    return jnp.stack(variants, axis=1)


def _rope_tables(T, rot, period):
    half = rot // 2
    inv = ROPE_THETA ** (-jnp.arange(half, dtype=F32) / half)
    ang = jnp.arange(T, dtype=F32)[:, None] * inv[None, :]
    cos, sin = jnp.cos(ang), jnp.sin(ang)
    zeros = jnp.zeros((T, half), F32)
    rest0 = jnp.zeros((T, period - rot), F32)
    c = jnp.concatenate([cos, cos, jnp.ones((T, period - rot), F32)], axis=1)
    sa = jnp.concatenate([zeros, sin, rest0], axis=1)
    sb = jnp.concatenate([-sin, zeros, rest0], axis=1)
    rep = LANES // period
    return tuple(jnp.tile(t, (1, rep)) for t in (c, sa, sb))


def _rope(x, c, sa, sb, half):
    return x * c + pltpu.roll(x, half, 1) * sa + pltpu.roll(x, LANES - half, 1) * sb


def _class_rows(x, n, dil):
    return pl.ds(x, n) if dil == 1 else pl.ds(x, n, stride=dil)


def _dil_block_plan(T):
    qb = DIL_Q_BLOCK
    plan = []
    for win, dil in DIL_PAIRS:
        radius = win // (2 * dil)
        n = T // dil
        if n > qb:
            kw = qb + 2 * radius
            blocks = []
            for q_start in range(0, T, qb):
                seg = (q_start // n) * n
                k_start = min(max(q_start - radius, seg), seg + n - kw)
                blocks.append((q_start, k_start, k_start - q_start))
        else:
            kw = qb
            blocks = [(q_start, q_start, 0) for q_start in range(0, T, qb)]
        plan.append(dict(dil=dil, n=n, radius=radius, kw=kw, blocks=blocks))
    return plan


def _dil_masks(T):
    qb = DIL_Q_BLOCK
    tables, index = [], {}
    for g, grp in enumerate(_dil_block_plan(T)):
        r = jnp.arange(qb)[:, None]
        c = jnp.arange(grp["kw"])[None, :]
        for _, _, off in grp["blocks"]:
            if (g, off) in index:
                continue
            ok = jnp.abs(c + off - r) <= grp["radius"]
            if grp["n"] < qb:
                ok = ok & (c // grp["n"] == r // grp["n"])
            index[(g, off)] = len(tables)
            tables.append(jnp.where(ok, 0.0, NEG).astype(F32))
    return tables, index


def _dil_kernel(*refs, T, mask_index):
    n_masks = len(set(mask_index.values()))
    q_refs = refs[:3]
    k_ref, v_ref, c_ref, sa_ref, sb_ref = refs[3:8]
    mask_refs = refs[8:8 + n_masks]
    o_ref, kr_ref, qr_ref, on_ref, ln_ref = refs[8 + n_masks:]
    half = ROT_DIM // 2
    qb = DIL_Q_BLOCK
    rope = lambda x: _rope(x, c_ref[...], sa_ref[...], sb_ref[...], half)
    kr_ref[...] = rope(k_ref[...])
    scale = HEAD_DIM ** -0.5 * LOG2E

    def class_major_pieces(start, length, grp):
        n, dil = grp["n"], grp["dil"]
        pos = start
        while pos < start + length:
            x, i = divmod(pos, n)
            take = min(n - i, start + length - pos)
            yield _class_rows(x + i * dil, take, dil), pos - start, take
            pos += take

    def gather(ref, start, length, grp):
        pieces = [ref[rows, :].astype(BF16) for rows, _, _ in class_major_pieces(start, length, grp)]
        return pieces[0] if len(pieces) == 1 else jnp.concatenate(pieces, axis=0)

    def scatter(ref, g, start, value, grp):
        for rows, off, take in class_major_pieces(start, value.shape[0], grp):
            ref[g, rows, :] = value[off:off + take]

    for g, grp in enumerate(_dil_block_plan(T)):
        qr_ref[...] = rope(q_refs[g][...]) * scale
        for q_start, k_start, off in grp["blocks"]:
            kw = grp["kw"]
            q = gather(qr_ref, q_start, qb, grp)
            k = gather(kr_ref, k_start, kw, grp)
            vx = jnp.concatenate([gather(v_ref, k_start, kw, grp), jnp.ones((kw, HEAD_DIM), BF16)], axis=1)
            s = _dot_nt(q, k) + mask_refs[mask_index[(g, off)]][...]
            m = jnp.max(s, axis=-1, keepdims=True)
            ox = _dot(jnp.exp2(s - m).astype(BF16), vx)
            l = ox[:, HEAD_DIM:]
            scatter(on_ref, g, q_start, ox[:, :HEAD_DIM] / l, grp)
            scatter(ln_ref, g, q_start, m * LN2 + jnp.log(l), grp)

    l0, l1, l2 = ln_ref[0], ln_ref[1], ln_ref[2]
    m = jnp.maximum(jnp.maximum(l0, l1), l2)
    e0, e1, e2 = jnp.exp(l0 - m), jnp.exp(l1 - m), jnp.exp(l2 - m)
    den = e0 + e1 + e2
    o = (e0 / den) * on_ref[0] + (e1 / den) * on_ref[1] + (e2 / den) * on_ref[2]
    o_ref[...] = o.astype(o_ref.dtype)


def _dilated(z, tables, masks, mask_index, cols):
    B, T, _ = z.shape
    q0, k0, v0 = cols
    G = len(DIL_PAIRS)
    assert G == 3 and all(T % (dil * GLA_CHUNK) == 0 for _, dil in DIL_PAIRS) and T % DIL_Q_BLOCK == 0
    zspec = lambda off: pl.BlockSpec((None, T, HEAD_DIM), lambda b, h: (b, 0, off // HEAD_DIM + h))
    const = lambda a: pl.BlockSpec(a.shape, lambda b, h: (0,) * a.ndim)
    return pl.pallas_call(
        functools.partial(_dil_kernel, T=T, mask_index=mask_index),
        grid=(B, DIL_HEADS),
        in_specs=([zspec(q0 + g * DIL_HEADS * HEAD_DIM) for g in range(G)] + [zspec(k0), zspec(v0)]
                  + [const(t) for t in tables] + [const(m) for m in masks]),
        out_specs=pl.BlockSpec((None, T, HEAD_DIM), lambda b, h: (b, 0, h)),
        out_shape=jax.ShapeDtypeStruct((B, T, DIL_HEADS * HEAD_DIM), BF16),
        scratch_shapes=[
            pltpu.VMEM((T, HEAD_DIM), F32), pltpu.VMEM((T, HEAD_DIM), F32),
            pltpu.VMEM((G, T, HEAD_DIM), F32), pltpu.VMEM((G, T, LANES), F32),
        ],
        compiler_params=_params("parallel", "arbitrary"),
        name="dilated_attn",
    )(z, z, z, z, z, *tables, *masks)


def _diff_kernel(q_ref, k_ref, v_ref, c_ref, sa_ref, sb_ref, lam_ref, sg_ref, o_ref, qb_ref, kb_ref, vb_ref,
                 score_a_ref, score_b_ref, *, lam_init):
    T = q_ref.shape[0]
    half = DIFF_DH // 8
    rope = lambda x: _rope(x, c_ref[...], sa_ref[...], sb_ref[...], half)
    qb_ref[...] = (rope(q_ref[...]) * (DIFF_DH ** -0.5 * LOG2E)).astype(BF16)
    kb_ref[...] = rope(k_ref[...]).astype(BF16)
    vb_ref[...] = v_ref[...].astype(BF16)
    lam = lam_ref[...]
    lam_full = (jnp.exp(jnp.sum(lam[0:1] * lam[1:2], axis=-1, keepdims=True))
                - jnp.exp(jnp.sum(lam[2:3] * lam[3:4], axis=-1, keepdims=True)) + lam_init)
    first = lax.broadcasted_iota(jnp.int32, (1, LANES), 1) < DIFF_DH
    qblk = ATTN_Q_BLOCK

    nblk = T // qblk
    assert nblk % 2 == 0

    def scores(i, s_ref):
        q = qb_ref[pl.ds(pl.multiple_of(i * qblk, qblk), qblk), :]
        zero = jnp.zeros_like(q)
        s_ref[0] = _dot_nt(jnp.where(first, q, zero), kb_ref[...])
        s_ref[1] = _dot_nt(jnp.where(first, zero, q), kb_ref[...])

    def exp_rows(s):
        e = jnp.exp2(s - jnp.max(s, axis=-1, keepdims=True))
        return e, jnp.sum(e, axis=-1, keepdims=True)

    def finish(i, s_ref):
        e0, l0 = exp_rows(s_ref[0])
        e1, l1 = exp_rows(s_ref[1])
        a = e0 * (1.0 / l0) - e1 * (lam_full / l1)
        o = _dot(a.astype(BF16), vb_ref[...])
        rows = pl.ds(pl.multiple_of(i * qblk, qblk), qblk)
        o_ref[rows, :] = (_rms(o, sg_ref[...]) * (1.0 - lam_init)).astype(o_ref.dtype)

    scores(0, score_a_ref)

    def body(j, carry):
        i = 2 * j
        scores(i + 1, score_b_ref)
        finish(i, score_a_ref)
        scores((i + 2) % nblk, score_a_ref)
        finish(i + 1, score_b_ref)
        return carry

    lax.fori_loop(0, nblk // 2, body, 0)


def _diff(z, tables, lam, subln_g, lam_init, cols):
    B, T, _ = z.shape
    q0, k0, v0 = cols
    W = 2 * DIFF_DH
    zspec = lambda off: pl.BlockSpec((None, T, W), lambda b, h: (b, 0, off // W + h))
    tspec = pl.BlockSpec((T, LANES), lambda b, h: (0, 0))
    return pl.pallas_call(
        functools.partial(_diff_kernel, lam_init=lam_init),
        grid=(B, DIFF_HEADS),
        in_specs=[
            zspec(q0), zspec(k0), zspec(v0), tspec, tspec, tspec,
            pl.BlockSpec(lam.shape, lambda b, h: (0, 0)),
            pl.BlockSpec((1, W), lambda b, h: (0, 0)),
        ],
        out_specs=pl.BlockSpec((None, T, W), lambda b, h: (b, 0, h)),
        out_shape=jax.ShapeDtypeStruct((B, T, DIFF_HEADS * W), BF16),
        scratch_shapes=[pltpu.VMEM((T, W), BF16), pltpu.VMEM((T, W), BF16), pltpu.VMEM((T, W), BF16),
                        pltpu.VMEM((2, ATTN_Q_BLOCK, T), F32), pltpu.VMEM((2, ATTN_Q_BLOCK, T), F32)],
        compiler_params=_params("parallel", "arbitrary"),
        name="diff_attn",
    )(z, z, z, *tables, lam, subln_g.reshape(1, W))


_EVEN_MAIN = 2 * GLA_HEADS * GLA_DK + 2 * GLA_HEADS * GLA_DV
_EVEN_NA = 3 * NA_HEADS * HEAD_DIM
_EVEN_COLS_GLA = (0, GLA_HEADS * GLA_DK, 2 * GLA_HEADS * GLA_DK, 2 * GLA_HEADS * GLA_DK + GLA_HEADS * GLA_DV,
                  _EVEN_MAIN + _EVEN_NA)
_EVEN_COLS_NA = tuple(_EVEN_MAIN + i * NA_HEADS * HEAD_DIM for i in range(3))
_ODD_Q = len(DIL_PAIRS) * DIL_HEADS * HEAD_DIM
_ODD_COLS_DIL = (0, _ODD_Q, _ODD_Q + DIL_HEADS * HEAD_DIM)
_ODD_COLS_DIFF = tuple(_ODD_Q + 2 * DIL_HEADS * HEAD_DIM + i * DIFF_HEADS * 2 * DIFF_DH for i in range(3))


def _prepare_even(w_in, w_out, wg_f, bg_f, wg_b, bg_b, rpb, T):
    D = w_in.shape[0]
    lr0 = _EVEN_MAIN
    lr1 = lr0 + 2 * GLA_RANK
    w = jnp.concatenate([w_in[:, :lr0], w_in[:, lr1:], w_in[:, lr0:lr1],
                         jnp.zeros((D, LANES - 2 * GLA_RANK), w_in.dtype)], axis=1).astype(BF16)
    pad = lambda a, before: jnp.pad(a, ((before, LANES - GLA_RANK - before), (0, 0))).astype(BF16)
    ya_width = GLA_HEADS * GLA_DV
    return dict(w_in=w, wa=w_out[:ya_width].astype(BF16), wb=w_out[ya_width:].astype(BF16),
                wgf=pad(wg_f, 0), wgb=pad(wg_b, GLA_RANK),
                bgf=bg_f.reshape(1, -1).astype(F32), bgb=bg_b.reshape(1, -1).astype(F32),
                bias=_na_bias_table(rpb, T))


def _trunk(x, p, w):
    B, T, D = x.shape
    M = B * T
    h = x.reshape(M, D)
    for i in range(DEPTH):
        g = w["norm_g"][i]
        j = i // 2
        if i % 2 == 0:
            e = w["even"][j]
            z = _norm_matmul(h, g[0], e["w_in"], F32).reshape(B, T, -1)
            ya = _gla(z, e["wgf"], e["bgf"], e["wgb"], e["bgb"], w["gla_norm_g"][j].reshape(1, -1), _EVEN_COLS_GLA)
            yb = _na(z, e["bias"], _EVEN_COLS_NA)
            wa, wb = e["wa"], e["wb"]
        else:
            o = w["odd"][j]
            z = _norm_matmul(h, g[0], o["w_in"], F32).reshape(B, T, -1)
            ya = _dilated(z, w["rope_dil"], *w["dil_masks"], _ODD_COLS_DIL)
            yb = _diff(z, w["rope_diff"], w["diff_lambda"][j], w["diff_subln_g"][j],
                       0.8 - 0.6 * math.exp(-0.3 * i), _ODD_COLS_DIFF)
            wa, wb = o["wa"], o["wb"]
        h = _out_proj(h, ya.reshape(M, -1), yb.reshape(M, -1), g[1], wa, wb)
        h = _ffn(h, g[2], g[3], w["ffn_gate"][i], w["ffn_up"][i], w["ffn_down"][i])
        h = _ple(h, p[i].reshape(M, -1), g[4], w["ple_proj"][i], w["ple_gate"][i])
    return h.reshape(B, T, D)


def kernel(x_prompt, x_sample, p_prompt, p_sample, norm_g, w_in_even, w_out_even, gla_wg_fwd, gla_bg_fwd, gla_wg_bwd, gla_bg_bwd, gla_norm_g, na_rpb, w_in_odd, w_out_odd, diff_lambda, diff_subln_g, w_ffn_gate, w_ffn_up, w_ffn_down, w_ple_proj, w_ple_gate):
    T = x_prompt.shape[1]
    assert x_sample.shape[1] == T
    yc_width = DIL_HEADS * HEAD_DIM
    w = dict(
        norm_g=norm_g.astype(F32),
        even=[_prepare_even(w_in_even[j], w_out_even[j], gla_wg_fwd[j], gla_bg_fwd[j], gla_wg_bwd[j],
                            gla_bg_bwd[j], na_rpb[j], T) for j in range(w_in_even.shape[0])],
        odd=[dict(w_in=w_in_odd[j].astype(BF16), wa=w_out_odd[j, :yc_width].astype(BF16),
                  wb=w_out_odd[j, yc_width:].astype(BF16)) for j in range(w_in_odd.shape[0])],
        gla_norm_g=gla_norm_g.astype(F32),
        diff_lambda=diff_lambda.astype(F32),
        diff_subln_g=diff_subln_g.astype(F32),
        rope_dil=_rope_tables(T, ROT_DIM, HEAD_DIM),
        dil_masks=_dil_masks(T),
        rope_diff=_rope_tables(T, DIFF_DH // 4, DIFF_DH),
        ffn_gate=w_ffn_gate.astype(BF16), ffn_up=w_ffn_up.astype(BF16), ffn_down=w_ffn_down.astype(BF16),
        ple_proj=w_ple_proj.astype(BF16), ple_gate=w_ple_gate.astype(BF16),
    )
    return (_trunk(x_prompt, p_prompt, w), _trunk(x_sample, p_sample, w))
```

```python
import functools
import math

import jax
import jax.numpy as jnp
from jax import lax
from jax.experimental import pallas as pl
from jax.experimental.pallas import tpu as pltpu

F32 = jnp.float32
BF16 = jnp.bfloat16

D_MODEL = 2048
DEPTH = 2
HEAD_DIM = 128
PLE_DIM = 256
GRID_W = 64
EPS = 1e-6
NEG = -1e30
LOG2E = math.log2(math.e)
LN2 = math.log(2.0)
ROPE_THETA = 500000.0
ROT_DIM = HEAD_DIM // 4
D_FF = 5632
GLA_HEADS = 4
GLA_DV = 256
GLA_DK = 128
GLA_RANK = 16
GLA_TAU = 16.0
GLA_CHUNK = 64
NA_HEADS = 8
NA_WIN_ROWS = 8
NA_WIN_COLS = 16
DIL_PAIRS = ((128, 1), (512, 4), (2048, 16))
DIL_HEADS = 8
DIFF_DH = 64
DIFF_HEADS = 8

LANES = 128
VMEM_LIMIT = 56 * 1024 * 1024

TOKEN_TILE = 512
FFN_TILE = 512
IN_PROJ_TILE = 1024
IN_PROJ_ROWS = 1024
GLA_BLOCK = 256
ATTN_Q_BLOCK = 256
DIL_Q_BLOCK = 128
NA_GROUP = 4
NA_GROUP_WIN = NA_WIN_ROWS + NA_GROUP - 1

NT_DIMS = (((1,), (1,)), ((), ()))
TN_DIMS = (((0,), (0,)), ((), ()))


def _params(*semantics):
    return pltpu.CompilerParams(dimension_semantics=semantics, vmem_limit_bytes=VMEM_LIMIT)


def _divisor_tile(n, cap):
    return max(t for t in range(LANES, cap + 1, LANES) if n % t == 0)


def _rms(x, g):
    return x * lax.rsqrt(jnp.mean(x * x, axis=-1, keepdims=True) + EPS) * g


def _sigmoid(x):
    return 1.0 / (1.0 + jnp.exp(-x))


def _dot(a, b):
    return jnp.dot(a, b, preferred_element_type=F32)


def _dot_nt(a, b):
    return lax.dot_general(a, b, NT_DIMS, preferred_element_type=F32)


def _norm_matmul_kernel(x_ref, g_ref, w_ref, o_ref, a_ref):
    @pl.when(pl.program_id(1) == 0)
    def _():
        a_ref[...] = _rms(x_ref[...], g_ref[...]).astype(BF16)

    o_ref[...] = _dot(a_ref[...], w_ref[...]).astype(o_ref.dtype)


def _norm_matmul(x, g, w, out_dtype):
    M, D = x.shape
    N = w.shape[1]
    tm, tn = IN_PROJ_ROWS, _divisor_tile(N, IN_PROJ_TILE)
    assert M % tm == 0
    return pl.pallas_call(
        _norm_matmul_kernel,
        grid=(M // tm, N // tn),
        in_specs=[
            pl.BlockSpec((tm, D), lambda i, j: (i, 0)),
            pl.BlockSpec((1, D), lambda i, j: (0, 0)),
            pl.BlockSpec((D, tn), lambda i, j: (0, j)),
        ],
        out_specs=pl.BlockSpec((tm, tn), lambda i, j: (i, j)),
        out_shape=jax.ShapeDtypeStruct((M, N), out_dtype),
        scratch_shapes=[pltpu.VMEM((tm, D), BF16)],
        compiler_params=_params("parallel", "arbitrary"),
        name="norm_in_proj",
    )(x, g.reshape(1, D), w)


def _out_proj_kernel(h_ref, ya_ref, yb_ref, g_ref, wa_ref, wb_ref, o_ref):
    m = _dot(ya_ref[...], wa_ref[...]) + _dot(yb_ref[...], wb_ref[...])
    o_ref[...] = h_ref[...] + _rms(m, g_ref[...])


def _out_proj(h, ya, yb, g, wa, wb):
    M, D = h.shape
    Ka, Kb = ya.shape[1], yb.shape[1]
    tm = TOKEN_TILE
    return pl.pallas_call(
        _out_proj_kernel,
        grid=(M // tm,),
        in_specs=[
            pl.BlockSpec((tm, D), lambda i: (i, 0)),
            pl.BlockSpec((tm, Ka), lambda i: (i, 0)),
            pl.BlockSpec((tm, Kb), lambda i: (i, 0)),
            pl.BlockSpec((1, D), lambda i: (0, 0)),
            pl.BlockSpec((Ka, D), lambda i: (0, 0)),
            pl.BlockSpec((Kb, D), lambda i: (0, 0)),
        ],
        out_specs=pl.BlockSpec((tm, D), lambda i: (i, 0)),
        out_shape=jax.ShapeDtypeStruct((M, D), F32),
        compiler_params=_params("parallel"),
        name="out_proj",
    )(h, ya, yb, g.reshape(1, D), wa, wb)


def _ffn_kernel(h_ref, g2_ref, g3_ref, wg_ref, wu_ref, wd_ref, o_ref, a_ref, acc_ref):
    f = pl.program_id(1)

    @pl.when(f == 0)
    def _():
        a_ref[...] = _rms(h_ref[...], g2_ref[...]).astype(BF16)
        acc_ref[...] = jnp.zeros_like(acc_ref)

    a = a_ref[...]
    gate = _dot(a, wg_ref[...])
    up = _dot(a, wu_ref[...])
    act = (gate * _sigmoid(gate) * up).astype(BF16)
    acc_ref[...] += _dot(act, wd_ref[...])

    @pl.when(f == pl.num_programs(1) - 1)
    def _():
        o_ref[...] = h_ref[...] + _rms(acc_ref[...], g3_ref[...])


def _ffn(h, g2, g3, wg, wu, wd):
    M, D = h.shape
    F = wg.shape[1]
    tm, tf = TOKEN_TILE, FFN_TILE
    assert M % tm == 0 and F % tf == 0
    return pl.pallas_call(
        _ffn_kernel,
        grid=(M // tm, F // tf),
        in_specs=[
            pl.BlockSpec((tm, D), lambda i, f: (i, 0)),
            pl.BlockSpec((1, D), lambda i, f: (0, 0)),
            pl.BlockSpec((1, D), lambda i, f: (0, 0)),
            pl.BlockSpec((D, tf), lambda i, f: (0, f)),
            pl.BlockSpec((D, tf), lambda i, f: (0, f)),
            pl.BlockSpec((tf, D), lambda i, f: (f, 0)),
        ],
        out_specs=pl.BlockSpec((tm, D), lambda i, f: (i, 0)),
        out_shape=jax.ShapeDtypeStruct((M, D), F32),
        scratch_shapes=[pltpu.VMEM((tm, D), BF16), pltpu.VMEM((tm, D), F32)],
        compiler_params=_params("parallel", "arbitrary"),
        name="ffn",
    )(h, g2.reshape(1, D), g3.reshape(1, D), wg, wu, wd)


def _ple_kernel(h_ref, p_ref, g_ref, wp_ref, wgate_ref, o_ref):
    h = h_ref[...]
    e = _rms(_dot(p_ref[...].astype(BF16), wp_ref[...]), g_ref[...])
    gate = _sigmoid(_dot(h.astype(BF16), wgate_ref[...]))
    o_ref[...] = h + e * gate


def _ple(h, p, g, wp, wgate):
    M, D = h.shape
    P = p.shape[1]
    tm = TOKEN_TILE
    return pl.pallas_call(
        _ple_kernel,
        grid=(M // tm,),
        in_specs=[
            pl.BlockSpec((tm, D), lambda i: (i, 0)),
            pl.BlockSpec((tm, P), lambda i: (i, 0)),
            pl.BlockSpec((1, D), lambda i: (0, 0)),
            pl.BlockSpec((P, D), lambda i: (0, 0)),
            pl.BlockSpec((D, D), lambda i: (0, 0)),
        ],
        out_specs=pl.BlockSpec((tm, D), lambda i: (i, 0)),
        out_shape=jax.ShapeDtypeStruct((M, D), F32),
        compiler_params=_params("parallel"),
        name="ple",
    )(h, p, g.reshape(1, D), wp, wgate)


def _split3(g):
    g1 = g.astype(BF16)
    r1 = g - g1.astype(F32)
    g2 = r1.astype(BF16)
    g3 = (r1 - g2.astype(F32)).astype(BF16)
    return g1, g2, g3


def _mask_dot(mask, parts):
    m = mask.astype(BF16)
    return _dot(m, parts[0]) + _dot(m, parts[1]) + _dot(m, parts[2])


def _log_sigmoid(x):
    return jnp.minimum(x, 0.0) - jnp.log1p(jnp.exp(-jnp.abs(x)))


def _gla_kernel(q_ref, k_ref, v_ref, r_ref, lr_ref, wgf_ref, bgf_ref, wgb_ref, bgb_ref, gn_ref, o_ref,
                vb_ref, intra_ref, qdf_ref, kuf_ref, totf_ref, interf_ref, sf_ref,
                qdb_ref, kub_ref, totb_ref, interb_ref, sb_ref):
    T = q_ref.shape[0]
    L = GLA_CHUNK
    N = T // L
    R = GLA_BLOCK
    row = lax.broadcasted_iota(jnp.int32, (R, R), 0)
    col = lax.broadcasted_iota(jnp.int32, (R, R), 1)
    same_chunk = (row // L) == (col // L)
    scale = GLA_DK ** -0.5
    vb_ref[...] = v_ref[...].astype(BF16)
    fwd = (wgf_ref, bgf_ref, same_chunk & (col <= row), same_chunk & (col <= row), L - 1, qdf_ref, kuf_ref, totf_ref)
    bwd = (wgb_ref, bgb_ref, same_chunk & (col >= row), same_chunk & (col > row), 0, qdb_ref, kub_ref, totb_ref)

    def local(blk, carry):
        rows = pl.ds(pl.multiple_of(blk * R, R), R)
        lr = lr_ref[rows, :].astype(BF16)
        q = q_ref[rows, :] * scale
        k = k_ref[rows, :]
        v = vb_ref[rows, :]
        intra = None
        for w_ref, b_ref, cum_mask, att_mask, last_row, qd_ref, ku_ref, tot_ref in (fwd, bwd):
            g = _split3(_log_sigmoid(_dot(lr, w_ref[...]) + b_ref[...]) / GLA_TAU)
            b = _mask_dot(cum_mask, g)
            b3 = b.reshape(R // L, L, GLA_DK)
            tot = jnp.broadcast_to(b3[:, last_row:last_row + 1, :], b3.shape).reshape(R, GLA_DK)
            qd = (q * jnp.exp(b)).astype(BF16)
            kd = (k * jnp.exp(-b)).astype(BF16)
            qd_ref[rows, :] = qd
            ku_ref[rows, :] = (k * jnp.exp(tot - b)).astype(BF16)
            tot_ref[rows, :] = tot
            part = _dot(jnp.where(att_mask, _dot_nt(qd, kd), 0.0).astype(BF16), v)
            intra = part if intra is None else intra + part
        intra_ref[rows, :] = intra
        return carry

    lax.fori_loop(0, T // R, local, 0, unroll=4)

    sf_ref[...] = jnp.zeros_like(sf_ref)
    sb_ref[...] = jnp.zeros_like(sb_ref)

    def scan(n, carry):
        for idx, qd_ref, ku_ref, tot_ref, inter_ref, s_ref in (
                (n, qdf_ref, kuf_ref, totf_ref, interf_ref, sf_ref),
                (N - 1 - n, qdb_ref, kub_ref, totb_ref, interb_ref, sb_ref)):
            start = pl.multiple_of(idx * L, L)
            rows = pl.ds(start, L)
            state = s_ref[...]
            inter_ref[rows, :] = _dot_nt(qd_ref[rows, :], state.astype(BF16))
            update = lax.dot_general(vb_ref[rows, :], ku_ref[rows, :], TN_DIMS, preferred_element_type=F32)
            s_ref[...] = state * jnp.exp(tot_ref[pl.ds(start, 1), :]) + update
        return carry

    lax.fori_loop(0, N, scan, 0, unroll=8)

    r = r_ref[...]
    o = _rms(intra_ref[...] + interf_ref[...] + interb_ref[...], gn_ref[...])
    o_ref[...] = (o * (r * _sigmoid(r))).astype(o_ref.dtype)


def _gla(z, wgf, bgf, wgb, bgb, gn, cols):
    B, T, _ = z.shape
    dk, dv = GLA_DK, GLA_DV
    q0, k0, v0, r0, lr0 = cols
    zspec = lambda width, off: pl.BlockSpec((None, T, width), lambda b, h: (b, 0, off // width + h))
    return pl.pallas_call(
        _gla_kernel,
        grid=(B, GLA_HEADS),
        in_specs=[
            zspec(dk, q0), zspec(dk, k0), zspec(dv, v0), zspec(dv, r0),
            pl.BlockSpec((None, T, LANES), lambda b, h: (b, 0, lr0 // LANES)),
            pl.BlockSpec((LANES, dk), lambda b, h: (0, h)),
            pl.BlockSpec((1, dk), lambda b, h: (0, h)),
            pl.BlockSpec((LANES, dk), lambda b, h: (0, h)),
            pl.BlockSpec((1, dk), lambda b, h: (0, h)),
            pl.BlockSpec((1, dv), lambda b, h: (0, 0)),
        ],
        out_specs=pl.BlockSpec((None, T, dv), lambda b, h: (b, 0, h)),
        out_shape=jax.ShapeDtypeStruct((B, T, GLA_HEADS * dv), BF16),
        scratch_shapes=[pltpu.VMEM((T, dv), BF16), pltpu.VMEM((T, dv), F32)] + 2 * [
            pltpu.VMEM((T, dk), BF16), pltpu.VMEM((T, dk), BF16), pltpu.VMEM((T, dk), F32),
            pltpu.VMEM((T, dv), F32), pltpu.VMEM((dv, dk), F32),
        ],
        compiler_params=_params("parallel", "arbitrary"),
        name="gla",
    )(z, z, z, z, z, wgf, bgf, wgb, bgb, gn)


def _na_window_start(gi, rows):
    lo, hi = 0, rows - NA_GROUP_WIN
    start = NA_GROUP * gi - NA_WIN_ROWS // 2
    return min(max(start, lo), hi) if isinstance(gi, int) else jnp.clip(start, lo, hi)


def _na_kernel(q_ref, k_ref, v_ref, bias_ref, o_ref, kb_ref, vx_ref):
    T = q_ref.shape[0]
    rows = T // GRID_W
    groups = rows // NA_GROUP
    qlen = NA_GROUP * GRID_W
    klen = NA_GROUP_WIN * GRID_W
    kb_ref[...] = k_ref[...].astype(BF16)
    vx_ref[:, :HEAD_DIM] = v_ref[...].astype(BF16)
    vx_ref[:, HEAD_DIM:] = jnp.ones((T, HEAD_DIM), BF16)
    scale = HEAD_DIM ** -0.5 * LOG2E

    def body(gi, carry):
        variant = jnp.where(gi == 0, 0, jnp.where(gi == groups - 1, 2, 1))
        qrows = pl.ds(pl.multiple_of(gi * qlen, qlen), qlen)
        krows = pl.ds(pl.multiple_of(_na_window_start(gi, rows) * GRID_W, GRID_W), klen)
        s = _dot_nt(q_ref[qrows, :].astype(BF16), kb_ref[krows, :]) * scale + bias_ref[variant]
        e = jnp.exp2(s - jnp.max(s, axis=-1, keepdims=True)).astype(BF16)
        ox = _dot(e, vx_ref[krows, :])
        o_ref[qrows, :] = (ox[:, :HEAD_DIM] / ox[:, HEAD_DIM:]).astype(o_ref.dtype)
        return carry

    lax.fori_loop(0, groups, body, 0, unroll=True)


def _na(z, bias, cols):
    B, T, _ = z.shape
    q0, k0, v0 = cols
    zspec = lambda off: pl.BlockSpec((None, T, HEAD_DIM), lambda b, h: (b, 0, off // HEAD_DIM + h))
    return pl.pallas_call(
        _na_kernel,
        grid=(B, NA_HEADS),
        in_specs=[
            zspec(q0), zspec(k0), zspec(v0),
            pl.BlockSpec((None,) + bias.shape[1:], lambda b, h: (h, 0, 0, 0)),
        ],
        out_specs=pl.BlockSpec((None, T, HEAD_DIM), lambda b, h: (b, 0, h)),
        out_shape=jax.ShapeDtypeStruct((B, T, NA_HEADS * HEAD_DIM), BF16),
        scratch_shapes=[pltpu.VMEM((T, HEAD_DIM), BF16), pltpu.VMEM((T, 2 * HEAD_DIM), BF16)],
        compiler_params=_params("parallel", "arbitrary"),
        name="neighbourhood_attn",
    )(z, z, z, bias)


def _na_bias_table(rpb, T):
    H = rpb.shape[0]
    W = GRID_W
    rows = T // W
    groups = rows // NA_GROUP
    assert rows % NA_GROUP == 0 and rows >= NA_GROUP_WIN and groups >= 3
    c = jnp.arange(W)
    c0 = jnp.clip(c - NA_WIN_COLS // 2, 0, W - NA_WIN_COLS)
    col_ok = (c[None, :] >= c0[:, None]) & (c[None, :] < c0[:, None] + NA_WIN_COLS)
    dc = c[None, :] - c[:, None] + NA_WIN_COLS - 1
    onehot = (dc[None] == jnp.arange(2 * NA_WIN_COLS - 1)[:, None, None]).astype(F32)
    by_dr = jnp.einsum("hrj,jqk->hrqk", rpb.astype(F32), onehot, precision=lax.Precision.HIGHEST)
    by_dr = jnp.where(col_ok, by_dr * LOG2E, NEG)
    masked = jnp.full((H, W, W), NEG, F32)
    variants = []
    for gi in (0, 1, groups - 1):
        ws = _na_window_start(gi, rows)
        per_row = []
        for ri in range(NA_GROUP):
            r = NA_GROUP * gi + ri
            kr0 = min(max(r - NA_WIN_ROWS // 2, 0), rows - NA_WIN_ROWS)
            blocks = []
            for o in range(NA_GROUP_WIN):
                key_row = ws + o
                ok = kr0 <= key_row < kr0 + NA_WIN_ROWS
                blocks.append(by_dr[:, key_row - r + NA_WIN_ROWS - 1] if ok else masked)
            per_row.append(jnp.concatenate(blocks, axis=-1))
        variants.append(jnp.concatenate(per_row, axis=1))
    return jnp.stack(variants, axis=1)


def _rope_tables(T, rot, period):
    half = rot // 2
    inv = ROPE_THETA ** (-jnp.arange(half, dtype=F32) / half)
    ang = jnp.arange(T, dtype=F32)[:, None] * inv[None, :]
    cos, sin = jnp.cos(ang), jnp.sin(ang)
    zeros = jnp.zeros((T, half), F32)
    rest0 = jnp.zeros((T, period - rot), F32)
    c = jnp.concatenate([cos, cos, jnp.ones((T, period - rot), F32)], axis=1)
    sa = jnp.concatenate([zeros, sin, rest0], axis=1)
    sb = jnp.concatenate([-sin, zeros, rest0], axis=1)
    rep = LANES // period
    return tuple(jnp.tile(t, (1, rep)) for t in (c, sa, sb))


def _rope(x, c, sa, sb, half):
    return x * c + pltpu.roll(x, half, 1) * sa + pltpu.roll(x, LANES - half, 1) * sb


def _class_rows(x, n, dil):
    return pl.ds(x, n) if dil == 1 else pl.ds(x, n, stride=dil)


def _dil_block_plan(T):
    qb = DIL_Q_BLOCK
    plan = []
    for win, dil in DIL_PAIRS:
        radius = win // (2 * dil)
        n = T // dil
        if n > qb:
            kw = qb + 2 * radius
            blocks = []
            for q_start in range(0, T, qb):
                seg = (q_start // n) * n
                k_start = min(max(q_start - radius, seg), seg + n - kw)
                blocks.append((q_start, k_start, k_start - q_start))
        else:
            kw = qb
            blocks = [(q_start, q_start, 0) for q_start in range(0, T, qb)]
        plan.append(dict(dil=dil, n=n, radius=radius, kw=kw, blocks=blocks))
    return plan


def _dil_masks(T):
    qb = DIL_Q_BLOCK
    tables, index = [], {}
    for g, grp in enumerate(_dil_block_plan(T)):
        r = jnp.arange(qb)[:, None]
        c = jnp.arange(grp["kw"])[None, :]
        for _, _, off in grp["blocks"]:
            if (g, off) in index:
                continue
            ok = jnp.abs(c + off - r) <= grp["radius"]
            if grp["n"] < qb:
                ok = ok & (c // grp["n"] == r // grp["n"])
            index[(g, off)] = len(tables)
            tables.append(jnp.where(ok, 0.0, NEG).astype(F32))
    return tables, index


def _dil_kernel(*refs, T, mask_index):
    n_masks = len(set(mask_index.values()))
    q_refs = refs[:3]
    k_ref, v_ref, c_ref, sa_ref, sb_ref = refs[3:8]
    mask_refs = refs[8:8 + n_masks]
    o_ref, kr_ref, qr_ref, qd_all_ref, kd_all_ref, vx_all_ref, on_ref, ln_ref = refs[8 + n_masks:]
    half = ROT_DIM // 2
    qb = DIL_Q_BLOCK
    rope = lambda x: _rope(x, c_ref[...], sa_ref[...], sb_ref[...], half)
    kr_ref[...] = rope(k_ref[...])
    vx_all_ref[:, :, HEAD_DIM:] = jnp.ones((len(DIL_PAIRS), T, HEAD_DIM), BF16)
    scale = HEAD_DIM ** -0.5 * LOG2E

    def scatter(ref, g, start, value, grp):
        n, dil = grp["n"], grp["dil"]
        pos = start
        while pos < start + value.shape[0]:
            x, i = divmod(pos, n)
            take = min(n - i, start + value.shape[0] - pos)
            ref[g, _class_rows(x + i * dil, take, dil), :] = value[pos - start:pos - start + take]
            pos += take

    for g, grp in enumerate(_dil_block_plan(T)):
        n, dil, kw = grp["n"], grp["dil"], grp["kw"]
        qd_ref, kd_ref, vx_ref = qd_all_ref.at[g], kd_all_ref.at[g], vx_all_ref.at[g]
        qr_ref[...] = rope(q_refs[g][...]) * scale
        for x in range(dil):
            src = _class_rows(x, n, dil)
            dst = pl.ds(x * n, n)
            qd_ref[dst, :] = qr_ref[src, :].astype(BF16)
            kd_ref[dst, :] = kr_ref[src, :].astype(BF16)
            vx_ref[dst, :HEAD_DIM] = v_ref[src, :].astype(BF16)
        for q_start, k_start, off in grp["blocks"]:
            krows = pl.ds(k_start, kw)
            s = _dot_nt(qd_ref[pl.ds(q_start, qb), :], kd_ref[krows, :]) + mask_refs[mask_index[(g, off)]][...]
            m = jnp.max(s, axis=-1, keepdims=True)
            ox = _dot(jnp.exp2(s - m).astype(BF16), vx_ref[krows, :])
            l = ox[:, HEAD_DIM:]
            scatter(on_ref, g, q_start, ox[:, :HEAD_DIM] / l, grp)
            scatter(ln_ref, g, q_start, m * LN2 + jnp.log(l), grp)

    l0, l1, l2 = ln_ref[0], ln_ref[1], ln_ref[2]
    m = jnp.maximum(jnp.maximum(l0, l1), l2)
    e0, e1, e2 = jnp.exp(l0 - m), jnp.exp(l1 - m), jnp.exp(l2 - m)
    den = e0 + e1 + e2
    o = (e0 / den) * on_ref[0] + (e1 / den) * on_ref[1] + (e2 / den) * on_ref[2]
    o_ref[...] = o.astype(o_ref.dtype)


def _dilated(z, tables, masks, mask_index, cols):
    B, T, _ = z.shape
    q0, k0, v0 = cols
    G = len(DIL_PAIRS)
    assert G == 3 and all(T % (dil * GLA_CHUNK) == 0 for _, dil in DIL_PAIRS) and T % DIL_Q_BLOCK == 0
    zspec = lambda off: pl.BlockSpec((None, T, HEAD_DIM), lambda b, h: (b, 0, off // HEAD_DIM + h))
    const = lambda a: pl.BlockSpec(a.shape, lambda b, h: (0,) * a.ndim)
    return pl.pallas_call(
        functools.partial(_dil_kernel, T=T, mask_index=mask_index),
        grid=(B, DIL_HEADS),
        in_specs=([zspec(q0 + g * DIL_HEADS * HEAD_DIM) for g in range(G)] + [zspec(k0), zspec(v0)]
                  + [const(t) for t in tables] + [const(m) for m in masks]),
        out_specs=pl.BlockSpec((None, T, HEAD_DIM), lambda b, h: (b, 0, h)),
        out_shape=jax.ShapeDtypeStruct((B, T, DIL_HEADS * HEAD_DIM), BF16),
        scratch_shapes=[
            pltpu.VMEM((T, HEAD_DIM), F32), pltpu.VMEM((T, HEAD_DIM), F32),
            pltpu.VMEM((G, T, HEAD_DIM), BF16), pltpu.VMEM((G, T, HEAD_DIM), BF16),
            pltpu.VMEM((G, T, 2 * HEAD_DIM), BF16),
            pltpu.VMEM((G, T, HEAD_DIM), F32), pltpu.VMEM((G, T, LANES), F32),
        ],
        compiler_params=_params("parallel", "arbitrary"),
        name="dilated_attn",
    )(z, z, z, z, z, *tables, *masks)


def _diff_kernel(q_ref, k_ref, v_ref, c_ref, sa_ref, sb_ref, lam_ref, sg_ref, o_ref, qb_ref, kb_ref, vb_ref,
                 score_a_ref, score_b_ref, *, lam_init):
    T = q_ref.shape[0]
    half = DIFF_DH // 8
    rope = lambda x: _rope(x, c_ref[...], sa_ref[...], sb_ref[...], half)
    qb_ref[...] = (rope(q_ref[...]) * (DIFF_DH ** -0.5 * LOG2E)).astype(BF16)
    kb_ref[...] = rope(k_ref[...]).astype(BF16)
    vb_ref[...] = v_ref[...].astype(BF16)
    lam = lam_ref[...]
    lam_full = (jnp.exp(jnp.sum(lam[0:1] * lam[1:2], axis=-1, keepdims=True))
                - jnp.exp(jnp.sum(lam[2:3] * lam[3:4], axis=-1, keepdims=True)) + lam_init)
    first = lax.broadcasted_iota(jnp.int32, (1, LANES), 1) < DIFF_DH
    qblk = ATTN_Q_BLOCK

    nblk = T // qblk
    assert nblk % 2 == 0

    def scores(i, s_ref):
        q = qb_ref[pl.ds(pl.multiple_of(i * qblk, qblk), qblk), :]
        zero = jnp.zeros_like(q)
        s_ref[0] = _dot_nt(jnp.where(first, q, zero), kb_ref[...])
        s_ref[1] = _dot_nt(jnp.where(first, zero, q), kb_ref[...])

    def exp_rows(s):
        e = jnp.exp2(s - jnp.max(s, axis=-1, keepdims=True))
        return e, jnp.sum(e, axis=-1, keepdims=True)

    def finish(i, s_ref):
        e0, l0 = exp_rows(s_ref[0])
        e1, l1 = exp_rows(s_ref[1])
        a = e0 * (1.0 / l0) - e1 * (lam_full / l1)
        o = _dot(a.astype(BF16), vb_ref[...])
        rows = pl.ds(pl.multiple_of(i * qblk, qblk), qblk)
        o_ref[rows, :] = (_rms(o, sg_ref[...]) * (1.0 - lam_init)).astype(o_ref.dtype)

    scores(0, score_a_ref)

    def body(j, carry):
        i = 2 * j
        scores(i + 1, score_b_ref)
        finish(i, score_a_ref)
        scores((i + 2) % nblk, score_a_ref)
        finish(i + 1, score_b_ref)
        return carry

    lax.fori_loop(0, nblk // 2, body, 0)


def _diff(z, tables, lam, subln_g, lam_init, cols):
    B, T, _ = z.shape
    q0, k0, v0 = cols
    W = 2 * DIFF_DH
    zspec = lambda off: pl.BlockSpec((None, T, W), lambda b, h: (b, 0, off // W + h))
    tspec = pl.BlockSpec((T, LANES), lambda b, h: (0, 0))
    return pl.pallas_call(
        functools.partial(_diff_kernel, lam_init=lam_init),
        grid=(B, DIFF_HEADS),
        in_specs=[
            zspec(q0), zspec(k0), zspec(v0), tspec, tspec, tspec,
            pl.BlockSpec(lam.shape, lambda b, h: (0, 0)),
            pl.BlockSpec((1, W), lambda b, h: (0, 0)),
        ],
        out_specs=pl.BlockSpec((None, T, W), lambda b, h: (b, 0, h)),
        out_shape=jax.ShapeDtypeStruct((B, T, DIFF_HEADS * W), BF16),
        scratch_shapes=[pltpu.VMEM((T, W), BF16), pltpu.VMEM((T, W), BF16), pltpu.VMEM((T, W), BF16),
                        pltpu.VMEM((2, ATTN_Q_BLOCK, T), F32), pltpu.VMEM((2, ATTN_Q_BLOCK, T), F32)],
        compiler_params=_params("parallel", "arbitrary"),
        name="diff_attn",
    )(z, z, z, *tables, lam, subln_g.reshape(1, W))


_EVEN_MAIN = 2 * GLA_HEADS * GLA_DK + 2 * GLA_HEADS * GLA_DV
_EVEN_NA = 3 * NA_HEADS * HEAD_DIM
_EVEN_COLS_GLA = (0, GLA_HEADS * GLA_DK, 2 * GLA_HEADS * GLA_DK, 2 * GLA_HEADS * GLA_DK + GLA_HEADS * GLA_DV,
                  _EVEN_MAIN + _EVEN_NA)
_EVEN_COLS_NA = tuple(_EVEN_MAIN + i * NA_HEADS * HEAD_DIM for i in range(3))
_ODD_Q = len(DIL_PAIRS) * DIL_HEADS * HEAD_DIM
_ODD_COLS_DIL = (0, _ODD_Q, _ODD_Q + DIL_HEADS * HEAD_DIM)
_ODD_COLS_DIFF = tuple(_ODD_Q + 2 * DIL_HEADS * HEAD_DIM + i * DIFF_HEADS * 2 * DIFF_DH for i in range(3))


def _prepare_even(w_in, w_out, wg_f, bg_f, wg_b, bg_b, rpb, T):
    D = w_in.shape[0]
    lr0 = _EVEN_MAIN
    lr1 = lr0 + 2 * GLA_RANK
    w = jnp.concatenate([w_in[:, :lr0], w_in[:, lr1:], w_in[:, lr0:lr1],
                         jnp.zeros((D, LANES - 2 * GLA_RANK), w_in.dtype)], axis=1).astype(BF16)
    pad = lambda a, before: jnp.pad(a, ((before, LANES - GLA_RANK - before), (0, 0))).astype(BF16)
    ya_width = GLA_HEADS * GLA_DV
    return dict(w_in=w, wa=w_out[:ya_width].astype(BF16), wb=w_out[ya_width:].astype(BF16),
                wgf=pad(wg_f, 0), wgb=pad(wg_b, GLA_RANK),
                bgf=bg_f.reshape(1, -1).astype(F32), bgb=bg_b.reshape(1, -1).astype(F32),
                bias=_na_bias_table(rpb, T))


def _trunk(x, p, w):
    B, T, D = x.shape
    M = B * T
    h = x.reshape(M, D)
    for i in range(DEPTH):
        g = w["norm_g"][i]
        j = i // 2
        if i % 2 == 0:
            e = w["even"][j]
            z = _norm_matmul(h, g[0], e["w_in"], F32).reshape(B, T, -1)
            ya = _gla(z, e["wgf"], e["bgf"], e["wgb"], e["bgb"], w["gla_norm_g"][j].reshape(1, -1), _EVEN_COLS_GLA)
            yb = _na(z, e["bias"], _EVEN_COLS_NA)
            wa, wb = e["wa"], e["wb"]
        else:
            o = w["odd"][j]
            z = _norm_matmul(h, g[0], o["w_in"], F32).reshape(B, T, -1)
            ya = _dilated(z, w["rope_dil"], *w["dil_masks"], _ODD_COLS_DIL)
            yb = _diff(z, w["rope_diff"], w["diff_lambda"][j], w["diff_subln_g"][j],
                       0.8 - 0.6 * math.exp(-0.3 * i), _ODD_COLS_DIFF)
            wa, wb = o["wa"], o["wb"]
        h = _out_proj(h, ya.reshape(M, -1), yb.reshape(M, -1), g[1], wa, wb)
        h = _ffn(h, g[2], g[3], w["ffn_gate"][i], w["ffn_up"][i], w["ffn_down"][i])
        h = _ple(h, p[i].reshape(M, -1), g[4], w["ple_proj"][i], w["ple_gate"][i])
    return h.reshape(B, T, D)


def kernel(x_prompt, x_sample, p_prompt, p_sample, norm_g, w_in_even, w_out_even, gla_wg_fwd, gla_bg_fwd, gla_wg_bwd, gla_bg_bwd, gla_norm_g, na_rpb, w_in_odd, w_out_odd, diff_lambda, diff_subln_g, w_ffn_gate, w_ffn_up, w_ffn_down, w_ple_proj, w_ple_gate):
    T = x_prompt.shape[1]
    assert x_sample.shape[1] == T
    yc_width = DIL_HEADS * HEAD_DIM
    w = dict(
        norm_g=norm_g.astype(F32),
        even=[_prepare_even(w_in_even[j], w_out_even[j], gla_wg_fwd[j], gla_bg_fwd[j], gla_wg_bwd[j],
                            gla_bg_bwd[j], na_rpb[j], T) for j in range(w_in_even.shape[0])],
        odd=[dict(w_in=w_in_odd[j].astype(BF16), wa=w_out_odd[j, :yc_width].astype(BF16),
                  wb=w_out_odd[j, yc_width:].astype(BF16)) for j in range(w_in_odd.shape[0])],
        gla_norm_g=gla_norm_g.astype(F32),
        diff_lambda=diff_lambda.astype(F32),
        diff_subln_g=diff_subln_g.astype(F32),
        rope_dil=_rope_tables(T, ROT_DIM, HEAD_DIM),
        dil_masks=_dil_masks(T),
        rope_diff=_rope_tables(T, DIFF_DH // 4, DIFF_DH),
        ffn_gate=w_ffn_gate.astype(BF16), ffn_up=w_ffn_up.astype(BF16), ffn_down=w_ffn_down.astype(BF16),
        ple_proj=w_ple_proj.astype(BF16), ple_gate=w_ple_gate.astype(BF16),
    )
    return (_trunk(x_prompt, p_prompt, w), _trunk(x_sample, p_sample, w))
```

```python
import functools
import math

import jax
import jax.numpy as jnp
from jax import lax
from jax.experimental import pallas as pl
from jax.experimental.pallas import tpu as pltpu

F32 = jnp.float32
BF16 = jnp.bfloat16

D_MODEL = 2048
DEPTH = 2
HEAD_DIM = 128
PLE_DIM = 256
GRID_W = 64
EPS = 1e-6
NEG = -1e30
LOG2E = math.log2(math.e)
LN2 = math.log(2.0)
ROPE_THETA = 500000.0
ROT_DIM = HEAD_DIM // 4
D_FF = 5632
GLA_HEADS = 4
GLA_DV = 256
GLA_DK = 128
GLA_RANK = 16
GLA_TAU = 16.0
GLA_CHUNK = 64
NA_HEADS = 8
NA_WIN_ROWS = 8
NA_WIN_COLS = 16
DIL_PAIRS = ((128, 1), (512, 4), (2048, 16))
DIL_HEADS = 8
DIFF_DH = 64
DIFF_HEADS = 8

LANES = 128
MXU_WIDTH = 256
VMEM_LIMIT = 56 * 1024 * 1024

TOKEN_TILE = 512
FFN_TILE = 512
IN_PROJ_TILE = 1280
IN_PROJ_ROWS = 1024
GLA_BLOCK = 256
ATTN_Q_BLOCK = 256
DIL_Q_BLOCK = 128
NA_GROUP = 4
NA_GROUP_WIN = NA_WIN_ROWS + NA_GROUP - 1

NT_DIMS = (((1,), (1,)), ((), ()))
TN_DIMS = (((0,), (0,)), ((), ()))


def _params(*semantics):
    return pltpu.CompilerParams(dimension_semantics=semantics, vmem_limit_bytes=VMEM_LIMIT)


def _divisor_tile(n, cap):
    return max(t for t in range(MXU_WIDTH, cap + 1, MXU_WIDTH) if n % t == 0)


def _rms(x, g):
    return x * lax.rsqrt(jnp.mean(x * x, axis=-1, keepdims=True) + EPS) * g


def _sigmoid(x):
    return 1.0 / (1.0 + jnp.exp(-x))


def _dot(a, b):
    return jnp.dot(a, b, preferred_element_type=F32)


def _dot_nt(a, b):
    return lax.dot_general(a, b, NT_DIMS, preferred_element_type=F32)


def _norm_matmul_kernel(x_ref, g_ref, w_ref, o_ref, a_ref):
    @pl.when(pl.program_id(1) == 0)
    def _():
        a_ref[...] = _rms(x_ref[...], g_ref[...]).astype(BF16)

    o_ref[...] = _dot(a_ref[...], w_ref[...]).astype(o_ref.dtype)


def _norm_matmul(x, g, w, out_dtype):
    M, D = x.shape
    N = w.shape[1]
    tm, tn = IN_PROJ_ROWS, _divisor_tile(N, IN_PROJ_TILE)
    assert M % tm == 0
    return pl.pallas_call(
        _norm_matmul_kernel,
        grid=(M // tm, N // tn),
        in_specs=[
            pl.BlockSpec((tm, D), lambda i, j: (i, 0)),
            pl.BlockSpec((1, D), lambda i, j: (0, 0)),
            pl.BlockSpec((D, tn), lambda i, j: (0, j)),
        ],
        out_specs=pl.BlockSpec((tm, tn), lambda i, j: (i, j)),
        out_shape=jax.ShapeDtypeStruct((M, N), out_dtype),
        scratch_shapes=[pltpu.VMEM((tm, D), BF16)],
        compiler_params=_params("parallel", "arbitrary"),
        name="norm_in_proj",
    )(x, g.reshape(1, D), w)


def _out_proj_kernel(h_ref, ya_ref, yb_ref, g_ref, wa_ref, wb_ref, o_ref):
    m = _dot(ya_ref[...], wa_ref[...]) + _dot(yb_ref[...], wb_ref[...])
    o_ref[...] = h_ref[...] + _rms(m, g_ref[...])


def _out_proj(h, ya, yb, g, wa, wb):
    M, D = h.shape
    Ka, Kb = ya.shape[1], yb.shape[1]
    tm = TOKEN_TILE
    return pl.pallas_call(
        _out_proj_kernel,
        grid=(M // tm,),
        in_specs=[
            pl.BlockSpec((tm, D), lambda i: (i, 0)),
            pl.BlockSpec((tm, Ka), lambda i: (i, 0)),
            pl.BlockSpec((tm, Kb), lambda i: (i, 0)),
            pl.BlockSpec((1, D), lambda i: (0, 0)),
            pl.BlockSpec((Ka, D), lambda i: (0, 0)),
            pl.BlockSpec((Kb, D), lambda i: (0, 0)),
        ],
        out_specs=pl.BlockSpec((tm, D), lambda i: (i, 0)),
        out_shape=jax.ShapeDtypeStruct((M, D), F32),
        compiler_params=_params("parallel"),
        name="out_proj",
    )(h, ya, yb, g.reshape(1, D), wa, wb)


def _ffn_kernel(h_ref, g2_ref, g3_ref, wg_ref, wu_ref, wd_ref, o_ref, a_ref, acc_ref):
    f = pl.program_id(1)

    @pl.when(f == 0)
    def _():
        a_ref[...] = _rms(h_ref[...], g2_ref[...]).astype(BF16)
        acc_ref[...] = jnp.zeros_like(acc_ref)

    a = a_ref[...]
    gate = _dot(a, wg_ref[...])
    up = _dot(a, wu_ref[...])
    act = (gate * _sigmoid(gate) * up).astype(BF16)
    acc_ref[...] += _dot(act, wd_ref[...])

    @pl.when(f == pl.num_programs(1) - 1)
    def _():
        o_ref[...] = h_ref[...] + _rms(acc_ref[...], g3_ref[...])


def _ffn(h, g2, g3, wg, wu, wd):
    M, D = h.shape
    F = wg.shape[1]
    tm, tf = TOKEN_TILE, FFN_TILE
    assert M % tm == 0 and F % tf == 0
    return pl.pallas_call(
        _ffn_kernel,
        grid=(M // tm, F // tf),
        in_specs=[
            pl.BlockSpec((tm, D), lambda i, f: (i, 0)),
            pl.BlockSpec((1, D), lambda i, f: (0, 0)),
            pl.BlockSpec((1, D), lambda i, f: (0, 0)),
            pl.BlockSpec((D, tf), lambda i, f: (0, f)),
            pl.BlockSpec((D, tf), lambda i, f: (0, f)),
            pl.BlockSpec((tf, D), lambda i, f: (f, 0)),
        ],
        out_specs=pl.BlockSpec((tm, D), lambda i, f: (i, 0)),
        out_shape=jax.ShapeDtypeStruct((M, D), F32),
        scratch_shapes=[pltpu.VMEM((tm, D), BF16), pltpu.VMEM((tm, D), F32)],
        compiler_params=_params("parallel", "arbitrary"),
        name="ffn",
    )(h, g2.reshape(1, D), g3.reshape(1, D), wg, wu, wd)


def _ple_kernel(h_ref, p_ref, g_ref, wp_ref, wgate_ref, o_ref):
    h = h_ref[...]
    e = _rms(_dot(p_ref[...].astype(BF16), wp_ref[...]), g_ref[...])
    gate = _sigmoid(_dot(h.astype(BF16), wgate_ref[...]))
    o_ref[...] = h + e * gate


def _ple(h, p, g, wp, wgate):
    M, D = h.shape
    P = p.shape[1]
    tm = TOKEN_TILE
    return pl.pallas_call(
        _ple_kernel,
        grid=(M // tm,),
        in_specs=[
            pl.BlockSpec((tm, D), lambda i: (i, 0)),
            pl.BlockSpec((tm, P), lambda i: (i, 0)),
            pl.BlockSpec((1, D), lambda i: (0, 0)),
            pl.BlockSpec((P, D), lambda i: (0, 0)),
            pl.BlockSpec((D, D), lambda i: (0, 0)),
        ],
        out_specs=pl.BlockSpec((tm, D), lambda i: (i, 0)),
        out_shape=jax.ShapeDtypeStruct((M, D), F32),
        compiler_params=_params("parallel"),
        name="ple",
    )(h, p, g.reshape(1, D), wp, wgate)


def _split3(g):
    g1 = g.astype(BF16)
    r1 = g - g1.astype(F32)
    g2 = r1.astype(BF16)
    g3 = (r1 - g2.astype(F32)).astype(BF16)
    return g1, g2, g3


def _mask_dot(mask, parts):
    m = mask.astype(BF16)
    return _dot(m, parts[0]) + _dot(m, parts[1]) + _dot(m, parts[2])


def _log_sigmoid(x):
    return jnp.minimum(x, 0.0) - jnp.log1p(jnp.exp(-jnp.abs(x)))


def _gla_kernel(q_ref, k_ref, v_ref, r_ref, lr_ref, wgf_ref, bgf_ref, wgb_ref, bgb_ref, gn_ref, o_ref,
                vb_ref, intra_ref, gate_ref, cum_ref, kd_ref, qdf_ref, kuf_ref, totf_ref, interf_ref, sf_ref,
                qdb_ref, kub_ref, totb_ref, interb_ref, sb_ref):
    T = q_ref.shape[0]
    L = GLA_CHUNK
    N = T // L
    R = GLA_BLOCK
    row = lax.broadcasted_iota(jnp.int32, (R, R), 0)
    col = lax.broadcasted_iota(jnp.int32, (R, R), 1)
    same_chunk = (row // L) == (col // L)
    scale = GLA_DK ** -0.5
    vb_ref[...] = v_ref[...].astype(BF16)
    fwd = (wgf_ref, bgf_ref, same_chunk & (col <= row), same_chunk & (col <= row), L - 1, qdf_ref, kuf_ref, totf_ref)
    bwd = (wgb_ref, bgb_ref, same_chunk & (col >= row), same_chunk & (col > row), 0, qdb_ref, kub_ref, totb_ref)

    blocks = [pl.ds(i * R, R) for i in range(T // R)]
    for rows in blocks:
        lr = lr_ref[rows, :].astype(BF16)
        for d, (w_ref, b_ref, *_) in enumerate((fwd, bwd)):
            pieces = _split3(_log_sigmoid(_dot(lr, w_ref[...]) + b_ref[...]) / GLA_TAU)
            for piece_index, piece in enumerate(pieces):
                gate_ref[d, piece_index, rows, :] = piece
    for rows in blocks:
        for d, (_, _, cum_mask, *_) in enumerate((fwd, bwd)):
            cum_ref[d, rows, :] = _mask_dot(cum_mask, [gate_ref[d, p, rows, :] for p in range(3)])
    for rows in blocks:
        q = q_ref[rows, :] * scale
        k = k_ref[rows, :]
        for d, (_, _, _, _, last_row, qd_ref, ku_ref, tot_ref) in enumerate((fwd, bwd)):
            b = cum_ref[d, rows, :]
            b3 = b.reshape(R // L, L, GLA_DK)
            tot = jnp.broadcast_to(b3[:, last_row:last_row + 1, :], b3.shape).reshape(R, GLA_DK)
            qd_ref[rows, :] = (q * jnp.exp(b)).astype(BF16)
            kd_ref[d, rows, :] = (k * jnp.exp(-b)).astype(BF16)
            ku_ref[rows, :] = (k * jnp.exp(tot - b)).astype(BF16)
            tot_ref[rows, :] = tot
    for rows in blocks:
        v = vb_ref[rows, :]
        intra = None
        for d, (_, _, _, att_mask, _, qd_ref, _, _) in enumerate((fwd, bwd)):
            att = jnp.where(att_mask, _dot_nt(qd_ref[rows, :], kd_ref[d, rows, :]), 0.0).astype(BF16)
            intra = _dot(att, v) if intra is None else intra + _dot(att, v)
        intra_ref[rows, :] = intra

    sf_ref[...] = jnp.zeros_like(sf_ref)
    sb_ref[...] = jnp.zeros_like(sb_ref)

    def scan(n, carry):
        for idx, qd_ref, ku_ref, tot_ref, inter_ref, s_ref in (
                (n, qdf_ref, kuf_ref, totf_ref, interf_ref, sf_ref),
                (N - 1 - n, qdb_ref, kub_ref, totb_ref, interb_ref, sb_ref)):
            start = pl.multiple_of(idx * L, L)
            rows = pl.ds(start, L)
            state = s_ref[...]
            inter_ref[rows, :] = _dot_nt(qd_ref[rows, :], state.astype(BF16))
            update = lax.dot_general(vb_ref[rows, :], ku_ref[rows, :], TN_DIMS, preferred_element_type=F32)
            s_ref[...] = state * jnp.exp(tot_ref[pl.ds(start, 1), :]) + update
        return carry

    lax.fori_loop(0, N, scan, 0, unroll=8)

    r = r_ref[...]
    o = _rms(intra_ref[...] + interf_ref[...] + interb_ref[...], gn_ref[...])
    o_ref[...] = (o * (r * _sigmoid(r))).astype(o_ref.dtype)


def _gla(z, wgf, bgf, wgb, bgb, gn, cols):
    B, T, _ = z.shape
    dk, dv = GLA_DK, GLA_DV
    q0, k0, v0, r0, lr0 = cols
    zspec = lambda width, off: pl.BlockSpec((None, T, width), lambda b, h: (b, 0, off // width + h))
    return pl.pallas_call(
        _gla_kernel,
        grid=(B, GLA_HEADS),
        in_specs=[
            zspec(dk, q0), zspec(dk, k0), zspec(dv, v0), zspec(dv, r0),
            pl.BlockSpec((None, T, LANES), lambda b, h: (b, 0, lr0 // LANES)),
            pl.BlockSpec((LANES, dk), lambda b, h: (0, h)),
            pl.BlockSpec((1, dk), lambda b, h: (0, h)),
            pl.BlockSpec((LANES, dk), lambda b, h: (0, h)),
            pl.BlockSpec((1, dk), lambda b, h: (0, h)),
            pl.BlockSpec((1, dv), lambda b, h: (0, 0)),
        ],
        out_specs=pl.BlockSpec((None, T, dv), lambda b, h: (b, 0, h)),
        out_shape=jax.ShapeDtypeStruct((B, T, GLA_HEADS * dv), BF16),
        scratch_shapes=[pltpu.VMEM((T, dv), BF16), pltpu.VMEM((T, dv), F32), pltpu.VMEM((2, 3, T, dk), BF16),
                        pltpu.VMEM((2, T, dk), F32), pltpu.VMEM((2, T, dk), BF16)] + 2 * [
            pltpu.VMEM((T, dk), BF16), pltpu.VMEM((T, dk), BF16), pltpu.VMEM((T, dk), F32),
            pltpu.VMEM((T, dv), F32), pltpu.VMEM((dv, dk), F32),
        ],
        compiler_params=_params("parallel", "arbitrary"),
        name="gla",
    )(z, z, z, z, z, wgf, bgf, wgb, bgb, gn)


def _na_window_start(gi, rows):
    lo, hi = 0, rows - NA_GROUP_WIN
    start = NA_GROUP * gi - NA_WIN_ROWS // 2
    return min(max(start, lo), hi) if isinstance(gi, int) else jnp.clip(start, lo, hi)


def _na_kernel(q_ref, k_ref, v_ref, bias_ref, o_ref, kb_ref, vx_ref):
    T = q_ref.shape[0]
    rows = T // GRID_W
    groups = rows // NA_GROUP
    qlen = NA_GROUP * GRID_W
    klen = NA_GROUP_WIN * GRID_W
    kb_ref[...] = k_ref[...].astype(BF16)
    vx_ref[:, :HEAD_DIM] = v_ref[...].astype(BF16)
    vx_ref[:, HEAD_DIM:] = jnp.ones((T, HEAD_DIM), BF16)
    scale = HEAD_DIM ** -0.5 * LOG2E

    def body(gi, carry):
        variant = jnp.where(gi == 0, 0, jnp.where(gi == groups - 1, 2, 1))
        qrows = pl.ds(pl.multiple_of(gi * qlen, qlen), qlen)
        krows = pl.ds(pl.multiple_of(_na_window_start(gi, rows) * GRID_W, GRID_W), klen)
        s = _dot_nt(q_ref[qrows, :].astype(BF16), kb_ref[krows, :]) * scale + bias_ref[variant]
        e = jnp.exp2(s - jnp.max(s, axis=-1, keepdims=True)).astype(BF16)
        ox = _dot(e, vx_ref[krows, :])
        o_ref[qrows, :] = (ox[:, :HEAD_DIM] / ox[:, HEAD_DIM:]).astype(o_ref.dtype)
        return carry

    lax.fori_loop(0, groups, body, 0, unroll=True)


def _na(z, bias, cols):
    B, T, _ = z.shape
    q0, k0, v0 = cols
    zspec = lambda off: pl.BlockSpec((None, T, HEAD_DIM), lambda b, h: (b, 0, off // HEAD_DIM + h))
    return pl.pallas_call(
        _na_kernel,
        grid=(B, NA_HEADS),
        in_specs=[
            zspec(q0), zspec(k0), zspec(v0),
            pl.BlockSpec((None,) + bias.shape[1:], lambda b, h: (h, 0, 0, 0)),
        ],
        out_specs=pl.BlockSpec((None, T, HEAD_DIM), lambda b, h: (b, 0, h)),
        out_shape=jax.ShapeDtypeStruct((B, T, NA_HEADS * HEAD_DIM), BF16),
        scratch_shapes=[pltpu.VMEM((T, HEAD_DIM), BF16), pltpu.VMEM((T, 2 * HEAD_DIM), BF16)],
        compiler_params=_params("parallel", "arbitrary"),
        name="neighbourhood_attn",
    )(z, z, z, bias)


def _na_bias_table(rpb, T):
    H = rpb.shape[0]
    W = GRID_W
    rows = T // W
    groups = rows // NA_GROUP
    assert rows % NA_GROUP == 0 and rows >= NA_GROUP_WIN and groups >= 3
    c = jnp.arange(W)
    c0 = jnp.clip(c - NA_WIN_COLS // 2, 0, W - NA_WIN_COLS)
    col_ok = (c[None, :] >= c0[:, None]) & (c[None, :] < c0[:, None] + NA_WIN_COLS)
    dc = c[None, :] - c[:, None] + NA_WIN_COLS - 1
    onehot = (dc[None] == jnp.arange(2 * NA_WIN_COLS - 1)[:, None, None]).astype(F32)
    by_dr = jnp.einsum("hrj,jqk->hrqk", rpb.astype(F32), onehot, precision=lax.Precision.HIGHEST)
    by_dr = jnp.where(col_ok, by_dr * LOG2E, NEG)
    masked = jnp.full((H, W, W), NEG, F32)
    variants = []
    for gi in (0, 1, groups - 1):
        ws = _na_window_start(gi, rows)
        per_row = []
        for ri in range(NA_GROUP):
            r = NA_GROUP * gi + ri
            kr0 = min(max(r - NA_WIN_ROWS // 2, 0), rows - NA_WIN_ROWS)
            blocks = []
            for o in range(NA_GROUP_WIN):
                key_row = ws + o
                ok = kr0 <= key_row < kr0 + NA_WIN_ROWS
                blocks.append(by_dr[:, key_row - r + NA_WIN_ROWS - 1] if ok else masked)
            per_row.append(jnp.concatenate(blocks, axis=-1))
        variants.append(jnp.concatenate(per_row, axis=1))
    return jnp.stack(variants, axis=1)


def _rope_tables(T, rot, period):
    half = rot // 2
    inv = ROPE_THETA ** (-jnp.arange(half, dtype=F32) / half)
    ang = jnp.arange(T, dtype=F32)[:, None] * inv[None, :]
    cos, sin = jnp.cos(ang), jnp.sin(ang)
    zeros = jnp.zeros((T, half), F32)
    rest0 = jnp.zeros((T, period - rot), F32)
    c = jnp.concatenate([cos, cos, jnp.ones((T, period - rot), F32)], axis=1)
    sa = jnp.concatenate([zeros, sin, rest0], axis=1)
    sb = jnp.concatenate([-sin, zeros, rest0], axis=1)
    rep = LANES // period
    return tuple(jnp.tile(t, (1, rep)) for t in (c, sa, sb))


def _rope(x, c, sa, sb, half):
    return x * c + pltpu.roll(x, half, 1) * sa + pltpu.roll(x, LANES - half, 1) * sb


def _class_rows(x, n, dil):
    return pl.ds(x, n) if dil == 1 else pl.ds(x, n, stride=dil)


def _dil_block_plan(T):
    qb = DIL_Q_BLOCK
    plan = []
    for win, dil in DIL_PAIRS:
        radius = win // (2 * dil)
        n = T // dil
        if n > qb:
            kw = qb + 2 * radius
            blocks = []
            for q_start in range(0, T, qb):
                seg = (q_start // n) * n
                k_start = min(max(q_start - radius, seg), seg + n - kw)
                blocks.append((q_start, k_start, k_start - q_start))
        else:
            kw = qb
            blocks = [(q_start, q_start, 0) for q_start in range(0, T, qb)]
        plan.append(dict(dil=dil, n=n, radius=radius, kw=kw, blocks=blocks))
    return plan


def _dil_masks(T):
    qb = DIL_Q_BLOCK
    tables, index = [], {}
    for g, grp in enumerate(_dil_block_plan(T)):
        r = jnp.arange(qb)[:, None]
        c = jnp.arange(grp["kw"])[None, :]
        for _, _, off in grp["blocks"]:
            if (g, off) in index:
                continue
            ok = jnp.abs(c + off - r) <= grp["radius"]
            if grp["n"] < qb:
                ok = ok & (c // grp["n"] == r // grp["n"])
            index[(g, off)] = len(tables)
            tables.append(jnp.where(ok, 0.0, NEG).astype(F32))
    return tables, index


def _dil_kernel(*refs, T, mask_index):
    n_masks = len(set(mask_index.values()))
    q_refs = refs[:3]
    k_ref, v_ref, c_ref, sa_ref, sb_ref = refs[3:8]
    mask_refs = refs[8:8 + n_masks]
    o_ref, kr_ref, qr_ref, qd_all_ref, kd_all_ref, vx_all_ref, on_ref, ln_ref = refs[8 + n_masks:]
    half = ROT_DIM // 2
    qb = DIL_Q_BLOCK
    rope = lambda x: _rope(x, c_ref[...], sa_ref[...], sb_ref[...], half)
    kr_ref[...] = rope(k_ref[...])
    vx_all_ref[:, :, HEAD_DIM:] = jnp.ones((len(DIL_PAIRS), T, HEAD_DIM), BF16)
    scale = HEAD_DIM ** -0.5 * LOG2E

    def scatter(ref, g, start, value, grp):
        n, dil = grp["n"], grp["dil"]
        pos = start
        while pos < start + value.shape[0]:
            x, i = divmod(pos, n)
            take = min(n - i, start + value.shape[0] - pos)
            ref[g, _class_rows(x + i * dil, take, dil), :] = value[pos - start:pos - start + take]
            pos += take

    for g, grp in enumerate(_dil_block_plan(T)):
        n, dil, kw = grp["n"], grp["dil"], grp["kw"]
        qd_ref, kd_ref, vx_ref = qd_all_ref.at[g], kd_all_ref.at[g], vx_all_ref.at[g]
        qr_ref[...] = rope(q_refs[g][...]) * scale
        for x in range(dil):
            src = _class_rows(x, n, dil)
            dst = pl.ds(x * n, n)
            qd_ref[dst, :] = qr_ref[src, :].astype(BF16)
            kd_ref[dst, :] = kr_ref[src, :].astype(BF16)
            vx_ref[dst, :HEAD_DIM] = v_ref[src, :].astype(BF16)
        for q_start, k_start, off in grp["blocks"]:
            krows = pl.ds(k_start, kw)
            s = _dot_nt(qd_ref[pl.ds(q_start, qb), :], kd_ref[krows, :]) + mask_refs[mask_index[(g, off)]][...]
            m = jnp.max(s, axis=-1, keepdims=True)
            ox = _dot(jnp.exp2(s - m).astype(BF16), vx_ref[krows, :])
            l = ox[:, HEAD_DIM:]
            scatter(on_ref, g, q_start, ox[:, :HEAD_DIM] / l, grp)
            scatter(ln_ref, g, q_start, m * LN2 + jnp.log(l), grp)

    l0, l1, l2 = ln_ref[0], ln_ref[1], ln_ref[2]
    m = jnp.maximum(jnp.maximum(l0, l1), l2)
    e0, e1, e2 = jnp.exp(l0 - m), jnp.exp(l1 - m), jnp.exp(l2 - m)
    den = e0 + e1 + e2
    o = (e0 / den) * on_ref[0] + (e1 / den) * on_ref[1] + (e2 / den) * on_ref[2]
    o_ref[...] = o.astype(o_ref.dtype)


def _dilated(z, tables, masks, mask_index, cols):
    B, T, _ = z.shape
    q0, k0, v0 = cols
    G = len(DIL_PAIRS)
    assert G == 3 and all(T % (dil * GLA_CHUNK) == 0 for _, dil in DIL_PAIRS) and T % DIL_Q_BLOCK == 0
    zspec = lambda off: pl.BlockSpec((None, T, HEAD_DIM), lambda b, h: (b, 0, off // HEAD_DIM + h))
    const = lambda a: pl.BlockSpec(a.shape, lambda b, h: (0,) * a.ndim)
    return pl.pallas_call(
        functools.partial(_dil_kernel, T=T, mask_index=mask_index),
        grid=(B, DIL_HEADS),
        in_specs=([zspec(q0 + g * DIL_HEADS * HEAD_DIM) for g in range(G)] + [zspec(k0), zspec(v0)]
                  + [const(t) for t in tables] + [const(m) for m in masks]),
        out_specs=pl.BlockSpec((None, T, HEAD_DIM), lambda b, h: (b, 0, h)),
        out_shape=jax.ShapeDtypeStruct((B, T, DIL_HEADS * HEAD_DIM), BF16),
        scratch_shapes=[
            pltpu.VMEM((T, HEAD_DIM), F32), pltpu.VMEM((T, HEAD_DIM), F32),
            pltpu.VMEM((G, T, HEAD_DIM), BF16), pltpu.VMEM((G, T, HEAD_DIM), BF16),
            pltpu.VMEM((G, T, 2 * HEAD_DIM), BF16),
            pltpu.VMEM((G, T, HEAD_DIM), F32), pltpu.VMEM((G, T, LANES), F32),
        ],
        compiler_params=_params("parallel", "arbitrary"),
        name="dilated_attn",
    )(z, z, z, z, z, *tables, *masks)


def _diff_kernel(q_ref, k_ref, v_ref, c_ref, sa_ref, sb_ref, lam_ref, sg_ref, o_ref, qb_ref, kb_ref, vb_ref,
                 score_a_ref, score_b_ref, *, lam_init):
    T = q_ref.shape[0]
    half = DIFF_DH // 8
    rope = lambda x: _rope(x, c_ref[...], sa_ref[...], sb_ref[...], half)
    qb_ref[...] = (rope(q_ref[...]) * (DIFF_DH ** -0.5 * LOG2E)).astype(BF16)
    kb_ref[...] = rope(k_ref[...]).astype(BF16)
    W = 2 * DIFF_DH
    vb_ref[:, :W] = v_ref[...].astype(BF16)
    vb_ref[:, W:] = jnp.ones((T, W), BF16)
    lam = lam_ref[...]
    lam_full = (jnp.exp(jnp.sum(lam[0:1] * lam[1:2], axis=-1, keepdims=True))
                - jnp.exp(jnp.sum(lam[2:3] * lam[3:4], axis=-1, keepdims=True)) + lam_init)
    first = lax.broadcasted_iota(jnp.int32, (1, LANES), 1) < DIFF_DH
    qblk = ATTN_Q_BLOCK

    nblk = T // qblk
    assert nblk % 2 == 0

    def scores(i, s_ref):
        q = qb_ref[pl.ds(pl.multiple_of(i * qblk, qblk), qblk), :]
        zero = jnp.zeros_like(q)
        s_ref[0] = _dot_nt(jnp.where(first, q, zero), kb_ref[...])
        s_ref[1] = _dot_nt(jnp.where(first, zero, q), kb_ref[...])

    def attend(s):
        ox = _dot(jnp.exp2(s - jnp.max(s, axis=-1, keepdims=True)).astype(BF16), vb_ref[...])
        return ox[:, :W] / ox[:, W:]

    def finish(i, s_ref):
        o = attend(s_ref[0]) - lam_full * attend(s_ref[1])
        rows = pl.ds(pl.multiple_of(i * qblk, qblk), qblk)
        o_ref[rows, :] = (_rms(o, sg_ref[...]) * (1.0 - lam_init)).astype(o_ref.dtype)

    scores(0, score_a_ref)

    def body(j, carry):
        i = 2 * j
        scores(i + 1, score_b_ref)
        finish(i, score_a_ref)
        scores((i + 2) % nblk, score_a_ref)
        finish(i + 1, score_b_ref)
        return carry

    lax.fori_loop(0, nblk // 2, body, 0)


def _diff(z, tables, lam, subln_g, lam_init, cols):
    B, T, _ = z.shape
    q0, k0, v0 = cols
    W = 2 * DIFF_DH
    zspec = lambda off: pl.BlockSpec((None, T, W), lambda b, h: (b, 0, off // W + h))
    tspec = pl.BlockSpec((T, LANES), lambda b, h: (0, 0))
    return pl.pallas_call(
        functools.partial(_diff_kernel, lam_init=lam_init),
        grid=(B, DIFF_HEADS),
        in_specs=[
            zspec(q0), zspec(k0), zspec(v0), tspec, tspec, tspec,
            pl.BlockSpec(lam.shape, lambda b, h: (0, 0)),
            pl.BlockSpec((1, W), lambda b, h: (0, 0)),
        ],
        out_specs=pl.BlockSpec((None, T, W), lambda b, h: (b, 0, h)),
        out_shape=jax.ShapeDtypeStruct((B, T, DIFF_HEADS * W), BF16),
        scratch_shapes=[pltpu.VMEM((T, W), BF16), pltpu.VMEM((T, W), BF16), pltpu.VMEM((T, 2 * W), BF16),
                        pltpu.VMEM((2, ATTN_Q_BLOCK, T), F32), pltpu.VMEM((2, ATTN_Q_BLOCK, T), F32)],
        compiler_params=_params("parallel", "arbitrary"),
        name="diff_attn",
    )(z, z, z, *tables, lam, subln_g.reshape(1, W))


_EVEN_MAIN = 2 * GLA_HEADS * GLA_DK + 2 * GLA_HEADS * GLA_DV
_EVEN_NA = 3 * NA_HEADS * HEAD_DIM
_EVEN_COLS_GLA = (0, GLA_HEADS * GLA_DK, 2 * GLA_HEADS * GLA_DK, 2 * GLA_HEADS * GLA_DK + GLA_HEADS * GLA_DV,
                  _EVEN_MAIN + _EVEN_NA)
_EVEN_COLS_NA = tuple(_EVEN_MAIN + i * NA_HEADS * HEAD_DIM for i in range(3))
_ODD_Q = len(DIL_PAIRS) * DIL_HEADS * HEAD_DIM
_ODD_COLS_DIL = (0, _ODD_Q, _ODD_Q + DIL_HEADS * HEAD_DIM)
_ODD_COLS_DIFF = tuple(_ODD_Q + 2 * DIL_HEADS * HEAD_DIM + i * DIFF_HEADS * 2 * DIFF_DH for i in range(3))


def _prepare_even(w_in, w_out, wg_f, bg_f, wg_b, bg_b, rpb, T):
    D = w_in.shape[0]
    lr0 = _EVEN_MAIN
    lr1 = lr0 + 2 * GLA_RANK
    w = jnp.concatenate([w_in[:, :lr0], w_in[:, lr1:], w_in[:, lr0:lr1],
                         jnp.zeros((D, MXU_WIDTH - 2 * GLA_RANK), w_in.dtype)], axis=1).astype(BF16)
    pad = lambda a, before: jnp.pad(a, ((before, LANES - GLA_RANK - before), (0, 0))).astype(BF16)
    ya_width = GLA_HEADS * GLA_DV
    return dict(w_in=w, wa=w_out[:ya_width].astype(BF16), wb=w_out[ya_width:].astype(BF16),
                wgf=pad(wg_f, 0), wgb=pad(wg_b, GLA_RANK),
                bgf=bg_f.reshape(1, -1).astype(F32), bgb=bg_b.reshape(1, -1).astype(F32),
                bias=_na_bias_table(rpb, T))


def _trunk(x, p, w):
    B, T, D = x.shape
    M = B * T
    h = x.reshape(M, D)
    for i in range(DEPTH):
        g = w["norm_g"][i]
        j = i // 2
        if i % 2 == 0:
            e = w["even"][j]
            z = _norm_matmul(h, g[0], e["w_in"], F32).reshape(B, T, -1)
            ya = _gla(z, e["wgf"], e["bgf"], e["wgb"], e["bgb"], w["gla_norm_g"][j].reshape(1, -1), _EVEN_COLS_GLA)
            yb = _na(z, e["bias"], _EVEN_COLS_NA)
            wa, wb = e["wa"], e["wb"]
        else:
            o = w["odd"][j]
            z = _norm_matmul(h, g[0], o["w_in"], F32).reshape(B, T, -1)
            ya = _dilated(z, w["rope_dil"], *w["dil_masks"], _ODD_COLS_DIL)
            yb = _diff(z, w["rope_diff"], w["diff_lambda"][j], w["diff_subln_g"][j],
                       0.8 - 0.6 * math.exp(-0.3 * i), _ODD_COLS_DIFF)
            wa, wb = o["wa"], o["wb"]
        h = _out_proj(h, ya.reshape(M, -1), yb.reshape(M, -1), g[1], wa, wb)
        h = _ffn(h, g[2], g[3], w["ffn_gate"][i], w["ffn_up"][i], w["ffn_down"][i])
        h = _ple(h, p[i].reshape(M, -1), g[4], w["ple_proj"][i], w["ple_gate"][i])
    return h.reshape(B, T, D)


def kernel(x_prompt, x_sample, p_prompt, p_sample, norm_g, w_in_even, w_out_even, gla_wg_fwd, gla_bg_fwd, gla_wg_bwd, gla_bg_bwd, gla_norm_g, na_rpb, w_in_odd, w_out_odd, diff_lambda, diff_subln_g, w_ffn_gate, w_ffn_up, w_ffn_down, w_ple_proj, w_ple_gate):
    T = x_prompt.shape[1]
    assert x_sample.shape[1] == T
    yc_width = DIL_HEADS * HEAD_DIM
    w = dict(
        norm_g=norm_g.astype(F32),
        even=[_prepare_even(w_in_even[j], w_out_even[j], gla_wg_fwd[j], gla_bg_fwd[j], gla_wg_bwd[j],
                            gla_bg_bwd[j], na_rpb[j], T) for j in range(w_in_even.shape[0])],
        odd=[dict(w_in=w_in_odd[j].astype(BF16), wa=w_out_odd[j, :yc_width].astype(BF16),
                  wb=w_out_odd[j, yc_width:].astype(BF16)) for j in range(w_in_odd.shape[0])],
        gla_norm_g=gla_norm_g.astype(F32),
        diff_lambda=diff_lambda.astype(F32),
        diff_subln_g=diff_subln_g.astype(F32),
        rope_dil=_rope_tables(T, ROT_DIM, HEAD_DIM),
        dil_masks=_dil_masks(T),
        rope_diff=_rope_tables(T, DIFF_DH // 4, DIFF_DH),
        ffn_gate=w_ffn_gate.astype(BF16), ffn_up=w_ffn_up.astype(BF16), ffn_down=w_ffn_down.astype(BF16),
        ple_proj=w_ple_proj.astype(BF16), ple_gate=w_ple_gate.astype(BF16),
    )
    return (_trunk(x_prompt, p_prompt, w), _trunk(x_sample, p_sample, w))
```

```python
import functools
import math

import jax
import jax.numpy as jnp
from jax import lax
from jax.experimental import pallas as pl
from jax.experimental.pallas import tpu as pltpu

F32 = jnp.float32
BF16 = jnp.bfloat16

D_MODEL = 2048
DEPTH = 2
HEAD_DIM = 128
PLE_DIM = 256
GRID_W = 64
EPS = 1e-6
NEG = -1e30
LOG2E = math.log2(math.e)
LN2 = math.log(2.0)
ROPE_THETA = 500000.0
ROT_DIM = HEAD_DIM // 4
D_FF = 5632
GLA_HEADS = 4
GLA_DV = 256
GLA_DK = 128
GLA_RANK = 16
GLA_TAU = 16.0
GLA_CHUNK = 64
NA_HEADS = 8
NA_WIN_ROWS = 8
NA_WIN_COLS = 16
DIL_PAIRS = ((128, 1), (512, 4), (2048, 16))
DIL_HEADS = 8
DIFF_DH = 64
DIFF_HEADS = 8

LANES = 128
MXU_WIDTH = 256
VMEM_LIMIT = 56 * 1024 * 1024

TOKEN_TILE = 512
FFN_TILE = 512
IN_PROJ_TILE = 1280
IN_PROJ_ROWS = 1024
GLA_BLOCK = 256
ATTN_Q_BLOCK = 512
DIL_Q_BLOCK = 128
NA_GROUP = 4
NA_GROUP_WIN = NA_WIN_ROWS + NA_GROUP - 1

NT_DIMS = (((1,), (1,)), ((), ()))
TN_DIMS = (((0,), (0,)), ((), ()))


def _params(*semantics):
    return pltpu.CompilerParams(dimension_semantics=semantics, vmem_limit_bytes=VMEM_LIMIT)


def _divisor_tile(n, cap):
    return max(t for t in range(MXU_WIDTH, cap + 1, MXU_WIDTH) if n % t == 0)


def _rms(x, g):
    return x * lax.rsqrt(jnp.mean(x * x, axis=-1, keepdims=True) + EPS) * g


def _sigmoid(x):
    return 1.0 / (1.0 + jnp.exp(-x))


def _dot(a, b):
    return jnp.dot(a, b, preferred_element_type=F32)


def _dot_nt(a, b):
    return lax.dot_general(a, b, NT_DIMS, preferred_element_type=F32)


def _norm_matmul_kernel(x_ref, g_ref, w_ref, o_ref, a_ref):
    @pl.when(pl.program_id(1) == 0)
    def _():
        a_ref[...] = _rms(x_ref[...], g_ref[...]).astype(BF16)

    o_ref[...] = _dot(a_ref[...], w_ref[...]).astype(o_ref.dtype)


def _norm_matmul(x, g, w, out_dtype):
    M, D = x.shape
    N = w.shape[1]
    tm, tn = IN_PROJ_ROWS, _divisor_tile(N, IN_PROJ_TILE)
    assert M % tm == 0
    return pl.pallas_call(
        _norm_matmul_kernel,
        grid=(M // tm, N // tn),
        in_specs=[
            pl.BlockSpec((tm, D), lambda i, j: (i, 0)),
            pl.BlockSpec((1, D), lambda i, j: (0, 0)),
            pl.BlockSpec((D, tn), lambda i, j: (0, j)),
        ],
        out_specs=pl.BlockSpec((tm, tn), lambda i, j: (i, j)),
        out_shape=jax.ShapeDtypeStruct((M, N), out_dtype),
        scratch_shapes=[pltpu.VMEM((tm, D), BF16)],
        compiler_params=_params("parallel", "arbitrary"),
        name="norm_in_proj",
    )(x, g.reshape(1, D), w)


def _out_proj_kernel(h_ref, ya_ref, yb_ref, g_ref, wa_ref, wb_ref, o_ref):
    m = _dot(ya_ref[...], wa_ref[...]) + _dot(yb_ref[...], wb_ref[...])
    o_ref[...] = h_ref[...] + _rms(m, g_ref[...])


def _out_proj(h, ya, yb, g, wa, wb):
    M, D = h.shape
    Ka, Kb = ya.shape[1], yb.shape[1]
    tm = TOKEN_TILE
    return pl.pallas_call(
        _out_proj_kernel,
        grid=(M // tm,),
        in_specs=[
            pl.BlockSpec((tm, D), lambda i: (i, 0)),
            pl.BlockSpec((tm, Ka), lambda i: (i, 0)),
            pl.BlockSpec((tm, Kb), lambda i: (i, 0)),
            pl.BlockSpec((1, D), lambda i: (0, 0)),
            pl.BlockSpec((Ka, D), lambda i: (0, 0)),
            pl.BlockSpec((Kb, D), lambda i: (0, 0)),
        ],
        out_specs=pl.BlockSpec((tm, D), lambda i: (i, 0)),
        out_shape=jax.ShapeDtypeStruct((M, D), F32),
        compiler_params=_params("parallel"),
        name="out_proj",
    )(h, ya, yb, g.reshape(1, D), wa, wb)


def _ffn_kernel(h_ref, g2_ref, g3_ref, wg_ref, wu_ref, wd_ref, o_ref, a_ref, acc_ref):
    f = pl.program_id(1)

    @pl.when(f == 0)
    def _():
        a_ref[...] = _rms(h_ref[...], g2_ref[...]).astype(BF16)
        acc_ref[...] = jnp.zeros_like(acc_ref)

    a = a_ref[...]
    gate = _dot(a, wg_ref[...])
    up = _dot(a, wu_ref[...])
    act = (gate * _sigmoid(gate) * up).astype(BF16)
    acc_ref[...] += _dot(act, wd_ref[...])

    @pl.when(f == pl.num_programs(1) - 1)
    def _():
        o_ref[...] = h_ref[...] + _rms(acc_ref[...], g3_ref[...])


def _ffn(h, g2, g3, wg, wu, wd):
    M, D = h.shape
    F = wg.shape[1]
    tm, tf = TOKEN_TILE, FFN_TILE
    assert M % tm == 0 and F % tf == 0
    return pl.pallas_call(
        _ffn_kernel,
        grid=(M // tm, F // tf),
        in_specs=[
            pl.BlockSpec((tm, D), lambda i, f: (i, 0)),
            pl.BlockSpec((1, D), lambda i, f: (0, 0)),
            pl.BlockSpec((1, D), lambda i, f: (0, 0)),
            pl.BlockSpec((D, tf), lambda i, f: (0, f)),
            pl.BlockSpec((D, tf), lambda i, f: (0, f)),
            pl.BlockSpec((tf, D), lambda i, f: (f, 0)),
        ],
        out_specs=pl.BlockSpec((tm, D), lambda i, f: (i, 0)),
        out_shape=jax.ShapeDtypeStruct((M, D), F32),
        scratch_shapes=[pltpu.VMEM((tm, D), BF16), pltpu.VMEM((tm, D), F32)],
        compiler_params=_params("parallel", "arbitrary"),
        name="ffn",
    )(h, g2.reshape(1, D), g3.reshape(1, D), wg, wu, wd)


def _ple_kernel(h_ref, p_ref, g_ref, wp_ref, wgate_ref, o_ref):
    h = h_ref[...]
    e = _rms(_dot(p_ref[...].astype(BF16), wp_ref[...]), g_ref[...])
    gate = _sigmoid(_dot(h.astype(BF16), wgate_ref[...]))
    o_ref[...] = h + e * gate


def _ple(h, p, g, wp, wgate):
    M, D = h.shape
    P = p.shape[1]
    tm = TOKEN_TILE
    return pl.pallas_call(
        _ple_kernel,
        grid=(M // tm,),
        in_specs=[
            pl.BlockSpec((tm, D), lambda i: (i, 0)),
            pl.BlockSpec((tm, P), lambda i: (i, 0)),
            pl.BlockSpec((1, D), lambda i: (0, 0)),
            pl.BlockSpec((P, D), lambda i: (0, 0)),
            pl.BlockSpec((D, D), lambda i: (0, 0)),
        ],
        out_specs=pl.BlockSpec((tm, D), lambda i: (i, 0)),
        out_shape=jax.ShapeDtypeStruct((M, D), F32),
        compiler_params=_params("parallel"),
        name="ple",
    )(h, p, g.reshape(1, D), wp, wgate)


def _split3(g):
    g1 = g.astype(BF16)
    r1 = g - g1.astype(F32)
    g2 = r1.astype(BF16)
    g3 = (r1 - g2.astype(F32)).astype(BF16)
    return g1, g2, g3


def _mask_dot(mask, parts):
    m = mask.astype(BF16)
    return _dot(m, parts[0]) + _dot(m, parts[1]) + _dot(m, parts[2])


def _log_sigmoid(x):
    return jnp.minimum(x, 0.0) - jnp.log1p(jnp.exp(-jnp.abs(x)))


def _gla_kernel(q_ref, k_ref, v_ref, r_ref, lr_ref, wgf_ref, bgf_ref, wgb_ref, bgb_ref, gn_ref, o_ref,
                vb_ref, intra_ref, gate_ref, cum_ref, kd_ref, qdf_ref, kuf_ref, totf_ref, interf_ref, sf_ref,
                qdb_ref, kub_ref, totb_ref, interb_ref, sb_ref):
    T = q_ref.shape[0]
    L = GLA_CHUNK
    N = T // L
    R = GLA_BLOCK
    row = lax.broadcasted_iota(jnp.int32, (R, R), 0)
    col = lax.broadcasted_iota(jnp.int32, (R, R), 1)
    same_chunk = (row // L) == (col // L)
    scale = GLA_DK ** -0.5
    vb_ref[...] = v_ref[...].astype(BF16)
    fwd = (wgf_ref, bgf_ref, same_chunk & (col <= row), same_chunk & (col <= row), L - 1, qdf_ref, kuf_ref, totf_ref)
    bwd = (wgb_ref, bgb_ref, same_chunk & (col >= row), same_chunk & (col > row), 0, qdb_ref, kub_ref, totb_ref)

    blocks = [pl.ds(i * R, R) for i in range(T // R)]
    for rows in blocks:
        lr = lr_ref[rows, :].astype(BF16)
        for d, (w_ref, b_ref, *_) in enumerate((fwd, bwd)):
            pieces = _split3(_log_sigmoid(_dot(lr, w_ref[...]) + b_ref[...]) / GLA_TAU)
            for piece_index, piece in enumerate(pieces):
                gate_ref[d, piece_index, rows, :] = piece
    for rows in blocks:
        for d, (_, _, cum_mask, *_) in enumerate((fwd, bwd)):
            cum_ref[d, rows, :] = _mask_dot(cum_mask, [gate_ref[d, p, rows, :] for p in range(3)])
    for rows in blocks:
        q = q_ref[rows, :] * scale
        k = k_ref[rows, :]
        for d, (_, _, _, _, last_row, qd_ref, ku_ref, tot_ref) in enumerate((fwd, bwd)):
            b = cum_ref[d, rows, :]
            b3 = b.reshape(R // L, L, GLA_DK)
            tot = jnp.broadcast_to(b3[:, last_row:last_row + 1, :], b3.shape).reshape(R, GLA_DK)
            qd_ref[rows, :] = (q * jnp.exp(b)).astype(BF16)
            kd_ref[d, rows, :] = (k * jnp.exp(-b)).astype(BF16)
            ku_ref[rows, :] = (k * jnp.exp(tot - b)).astype(BF16)
            tot_ref[rows, :] = tot
    for rows in blocks:
        v = vb_ref[rows, :]
        intra = None
        for d, (_, _, _, att_mask, _, qd_ref, _, _) in enumerate((fwd, bwd)):
            att = jnp.where(att_mask, _dot_nt(qd_ref[rows, :], kd_ref[d, rows, :]), 0.0).astype(BF16)
            intra = _dot(att, v) if intra is None else intra + _dot(att, v)
        intra_ref[rows, :] = intra

    sf_ref[...] = jnp.zeros_like(sf_ref)
    sb_ref[...] = jnp.zeros_like(sb_ref)

    def scan(n, carry):
        for idx, qd_ref, ku_ref, tot_ref, inter_ref, s_ref in (
                (n, qdf_ref, kuf_ref, totf_ref, interf_ref, sf_ref),
                (N - 1 - n, qdb_ref, kub_ref, totb_ref, interb_ref, sb_ref)):
            start = pl.multiple_of(idx * L, L)
            rows = pl.ds(start, L)
            state = s_ref[...]
            inter_ref[rows, :] = _dot_nt(qd_ref[rows, :], state.astype(BF16))
            update = lax.dot_general(vb_ref[rows, :], ku_ref[rows, :], TN_DIMS, preferred_element_type=F32)
            s_ref[...] = state * jnp.exp(tot_ref[pl.ds(start, 1), :]) + update
        return carry

    lax.fori_loop(0, N, scan, 0, unroll=8)

    r = r_ref[...]
    o = _rms(intra_ref[...] + interf_ref[...] + interb_ref[...], gn_ref[...])
    o_ref[...] = (o * (r * _sigmoid(r))).astype(o_ref.dtype)


def _gla(z, wgf, bgf, wgb, bgb, gn, cols):
    B, T, _ = z.shape
    dk, dv = GLA_DK, GLA_DV
    q0, k0, v0, r0, lr0 = cols
    zspec = lambda width, off: pl.BlockSpec((None, T, width), lambda b, h: (b, 0, off // width + h))
    return pl.pallas_call(
        _gla_kernel,
        grid=(B, GLA_HEADS),
        in_specs=[
            zspec(dk, q0), zspec(dk, k0), zspec(dv, v0), zspec(dv, r0),
            pl.BlockSpec((None, T, LANES), lambda b, h: (b, 0, lr0 // LANES)),
            pl.BlockSpec((LANES, dk), lambda b, h: (0, h)),
            pl.BlockSpec((1, dk), lambda b, h: (0, h)),
            pl.BlockSpec((LANES, dk), lambda b, h: (0, h)),
            pl.BlockSpec((1, dk), lambda b, h: (0, h)),
            pl.BlockSpec((1, dv), lambda b, h: (0, 0)),
        ],
        out_specs=pl.BlockSpec((None, T, dv), lambda b, h: (b, 0, h)),
        out_shape=jax.ShapeDtypeStruct((B, T, GLA_HEADS * dv), BF16),
        scratch_shapes=[pltpu.VMEM((T, dv), BF16), pltpu.VMEM((T, dv), F32), pltpu.VMEM((2, 3, T, dk), BF16),
                        pltpu.VMEM((2, T, dk), F32), pltpu.VMEM((2, T, dk), BF16)] + 2 * [
            pltpu.VMEM((T, dk), BF16), pltpu.VMEM((T, dk), BF16), pltpu.VMEM((T, dk), F32),
            pltpu.VMEM((T, dv), F32), pltpu.VMEM((dv, dk), F32),
        ],
        compiler_params=_params("parallel", "arbitrary"),
        name="gla",
    )(z, z, z, z, z, wgf, bgf, wgb, bgb, gn)


def _na_window_start(gi, rows):
    lo, hi = 0, rows - NA_GROUP_WIN
    start = NA_GROUP * gi - NA_WIN_ROWS // 2
    return min(max(start, lo), hi) if isinstance(gi, int) else jnp.clip(start, lo, hi)


def _na_kernel(q_ref, k_ref, v_ref, bias_ref, o_ref, kb_ref, vx_ref):
    T = q_ref.shape[0]
    rows = T // GRID_W
    groups = rows // NA_GROUP
    qlen = NA_GROUP * GRID_W
    klen = NA_GROUP_WIN * GRID_W
    kb_ref[...] = k_ref[...].astype(BF16)
    vx_ref[:, :HEAD_DIM] = v_ref[...].astype(BF16)
    vx_ref[:, HEAD_DIM:] = jnp.ones((T, HEAD_DIM), BF16)
    scale = HEAD_DIM ** -0.5 * LOG2E

    def body(gi, carry):
        variant = jnp.where(gi == 0, 0, jnp.where(gi == groups - 1, 2, 1))
        qrows = pl.ds(pl.multiple_of(gi * qlen, qlen), qlen)
        krows = pl.ds(pl.multiple_of(_na_window_start(gi, rows) * GRID_W, GRID_W), klen)
        s = _dot_nt(q_ref[qrows, :].astype(BF16), kb_ref[krows, :]) * scale + bias_ref[variant]
        e = jnp.exp2(s - jnp.max(s, axis=-1, keepdims=True)).astype(BF16)
        ox = _dot(e, vx_ref[krows, :])
        o_ref[qrows, :] = (ox[:, :HEAD_DIM] / ox[:, HEAD_DIM:]).astype(o_ref.dtype)
        return carry

    lax.fori_loop(0, groups, body, 0, unroll=True)


def _na(z, bias, cols):
    B, T, _ = z.shape
    q0, k0, v0 = cols
    zspec = lambda off: pl.BlockSpec((None, T, HEAD_DIM), lambda b, h: (b, 0, off // HEAD_DIM + h))
    return pl.pallas_call(
        _na_kernel,
        grid=(B, NA_HEADS),
        in_specs=[
            zspec(q0), zspec(k0), zspec(v0),
            pl.BlockSpec((None,) + bias.shape[1:], lambda b, h: (h, 0, 0, 0)),
        ],
        out_specs=pl.BlockSpec((None, T, HEAD_DIM), lambda b, h: (b, 0, h)),
        out_shape=jax.ShapeDtypeStruct((B, T, NA_HEADS * HEAD_DIM), BF16),
        scratch_shapes=[pltpu.VMEM((T, HEAD_DIM), BF16), pltpu.VMEM((T, 2 * HEAD_DIM), BF16)],
        compiler_params=_params("parallel", "arbitrary"),
        name="neighbourhood_attn",
    )(z, z, z, bias)


def _na_bias_table(rpb, T):
    H = rpb.shape[0]
    W = GRID_W
    rows = T // W
    groups = rows // NA_GROUP
    assert rows % NA_GROUP == 0 and rows >= NA_GROUP_WIN and groups >= 3
    c = jnp.arange(W)
    c0 = jnp.clip(c - NA_WIN_COLS // 2, 0, W - NA_WIN_COLS)
    col_ok = (c[None, :] >= c0[:, None]) & (c[None, :] < c0[:, None] + NA_WIN_COLS)
    dc = c[None, :] - c[:, None] + NA_WIN_COLS - 1
    onehot = (dc[None] == jnp.arange(2 * NA_WIN_COLS - 1)[:, None, None]).astype(F32)
    by_dr = jnp.einsum("hrj,jqk->hrqk", rpb.astype(F32), onehot, precision=lax.Precision.HIGHEST)
    by_dr = jnp.where(col_ok, by_dr * LOG2E, NEG)
    masked = jnp.full((H, W, W), NEG, F32)
    variants = []
    for gi in (0, 1, groups - 1):
        ws = _na_window_start(gi, rows)
        per_row = []
        for ri in range(NA_GROUP):
            r = NA_GROUP * gi + ri
            kr0 = min(max(r - NA_WIN_ROWS // 2, 0), rows - NA_WIN_ROWS)
            blocks = []
            for o in range(NA_GROUP_WIN):
                key_row = ws + o
                ok = kr0 <= key_row < kr0 + NA_WIN_ROWS
                blocks.append(by_dr[:, key_row - r + NA_WIN_ROWS - 1] if ok else masked)
            per_row.append(jnp.concatenate(blocks, axis=-1))
        variants.append(jnp.concatenate(per_row, axis=1))
    return jnp.stack(variants, axis=1)


def _rope_tables(T, rot, period):
    half = rot // 2
    inv = ROPE_THETA ** (-jnp.arange(half, dtype=F32) / half)
    ang = jnp.arange(T, dtype=F32)[:, None] * inv[None, :]
    cos, sin = jnp.cos(ang), jnp.sin(ang)
    zeros = jnp.zeros((T, half), F32)
    rest0 = jnp.zeros((T, period - rot), F32)
    c = jnp.concatenate([cos, cos, jnp.ones((T, period - rot), F32)], axis=1)
    sa = jnp.concatenate([zeros, sin, rest0], axis=1)
    sb = jnp.concatenate([-sin, zeros, rest0], axis=1)
    rep = LANES // period
    return tuple(jnp.tile(t, (1, rep)) for t in (c, sa, sb))


def _rope(x, c, sa, sb, half):
    return x * c + pltpu.roll(x, half, 1) * sa + pltpu.roll(x, LANES - half, 1) * sb


def _class_rows(x, n, dil):
    return pl.ds(x, n) if dil == 1 else pl.ds(x, n, stride=dil)


def _dil_block_plan(T):
    qb = DIL_Q_BLOCK
    plan = []
    for win, dil in DIL_PAIRS:
        radius = win // (2 * dil)
        n = T // dil
        if n >= qb:
            kw = min(n, qb + 2 * radius)
            blocks = []
            for q_start in range(0, T, qb):
                seg = (q_start // n) * n
                k_start = min(max(q_start - radius, seg), seg + n - kw)
                blocks.append((q_start, k_start, k_start - q_start))
        else:
            kw = qb
            blocks = [(q_start, q_start, 0) for q_start in range(0, T, qb)]
        plan.append(dict(dil=dil, n=n, radius=radius, kw=kw, blocks=blocks))
    return plan


def _dil_masks(T):
    qb = DIL_Q_BLOCK
    tables, index = [], {}
    for g, grp in enumerate(_dil_block_plan(T)):
        r = jnp.arange(qb)[:, None]
        c = jnp.arange(grp["kw"])[None, :]
        for _, _, off in grp["blocks"]:
            if (g, off) in index:
                continue
            ok = jnp.abs(c + off - r) <= grp["radius"]
            if grp["n"] < qb:
                ok = ok & (c // grp["n"] == r // grp["n"])
            index[(g, off)] = len(tables)
            tables.append(jnp.where(ok, 0.0, NEG).astype(F32))
    return tables, index


def _dil_kernel(*refs, T, mask_index):
    n_masks = len(set(mask_index.values()))
    q_refs = refs[:3]
    k_ref, v_ref, c_ref, sa_ref, sb_ref = refs[3:8]
    mask_refs = refs[8:8 + n_masks]
    o_ref, kr_ref, qr_ref, qd_all_ref, kd_all_ref, vx_all_ref, on_ref, ln_ref = refs[8 + n_masks:]
    half = ROT_DIM // 2
    qb = DIL_Q_BLOCK
    rope = lambda x: _rope(x, c_ref[...], sa_ref[...], sb_ref[...], half)
    kr_ref[...] = rope(k_ref[...])
    vx_all_ref[:, :, HEAD_DIM:] = jnp.ones((len(DIL_PAIRS), T, HEAD_DIM), BF16)
    scale = HEAD_DIM ** -0.5 * LOG2E

    def scatter(ref, g, start, value, grp):
        n, dil = grp["n"], grp["dil"]
        pos = start
        while pos < start + value.shape[0]:
            x, i = divmod(pos, n)
            take = min(n - i, start + value.shape[0] - pos)
            ref[g, _class_rows(x + i * dil, take, dil), :] = value[pos - start:pos - start + take]
            pos += take

    for g, grp in enumerate(_dil_block_plan(T)):
        n, dil, kw = grp["n"], grp["dil"], grp["kw"]
        qd_ref, kd_ref, vx_ref = qd_all_ref.at[g], kd_all_ref.at[g], vx_all_ref.at[g]
        qr_ref[...] = rope(q_refs[g][...]) * scale
        for x in range(dil):
            src = _class_rows(x, n, dil)
            dst = pl.ds(x * n, n)
            qd_ref[dst, :] = qr_ref[src, :].astype(BF16)
            kd_ref[dst, :] = kr_ref[src, :].astype(BF16)
            vx_ref[dst, :HEAD_DIM] = v_ref[src, :].astype(BF16)
        for q_start, k_start, off in grp["blocks"]:
            krows = pl.ds(k_start, kw)
            s = _dot_nt(qd_ref[pl.ds(q_start, qb), :], kd_ref[krows, :]) + mask_refs[mask_index[(g, off)]][...]
            m = jnp.max(s, axis=-1, keepdims=True)
            ox = _dot(jnp.exp2(s - m).astype(BF16), vx_ref[krows, :])
            l = ox[:, HEAD_DIM:]
            scatter(on_ref, g, q_start, ox[:, :HEAD_DIM] / l, grp)
            scatter(ln_ref, g, q_start, m * LN2 + jnp.log(l), grp)

    l0, l1, l2 = ln_ref[0], ln_ref[1], ln_ref[2]
    m = jnp.maximum(jnp.maximum(l0, l1), l2)
    e0, e1, e2 = jnp.exp(l0 - m), jnp.exp(l1 - m), jnp.exp(l2 - m)
    den = e0 + e1 + e2
    o = (e0 / den) * on_ref[0] + (e1 / den) * on_ref[1] + (e2 / den) * on_ref[2]
    o_ref[...] = o.astype(o_ref.dtype)


def _dilated(z, tables, masks, mask_index, cols):
    B, T, _ = z.shape
    q0, k0, v0 = cols
    G = len(DIL_PAIRS)
    assert G == 3 and all(T % (dil * GLA_CHUNK) == 0 for _, dil in DIL_PAIRS) and T % DIL_Q_BLOCK == 0
    zspec = lambda off: pl.BlockSpec((None, T, HEAD_DIM), lambda b, h: (b, 0, off // HEAD_DIM + h))
    const = lambda a: pl.BlockSpec(a.shape, lambda b, h: (0,) * a.ndim)
    return pl.pallas_call(
        functools.partial(_dil_kernel, T=T, mask_index=mask_index),
        grid=(B, DIL_HEADS),
        in_specs=([zspec(q0 + g * DIL_HEADS * HEAD_DIM) for g in range(G)] + [zspec(k0), zspec(v0)]
                  + [const(t) for t in tables] + [const(m) for m in masks]),
        out_specs=pl.BlockSpec((None, T, HEAD_DIM), lambda b, h: (b, 0, h)),
        out_shape=jax.ShapeDtypeStruct((B, T, DIL_HEADS * HEAD_DIM), BF16),
        scratch_shapes=[
            pltpu.VMEM((T, HEAD_DIM), F32), pltpu.VMEM((T, HEAD_DIM), F32),
            pltpu.VMEM((G, T, HEAD_DIM), BF16), pltpu.VMEM((G, T, HEAD_DIM), BF16),
            pltpu.VMEM((G, T, 2 * HEAD_DIM), BF16),
            pltpu.VMEM((G, T, HEAD_DIM), F32), pltpu.VMEM((G, T, LANES), F32),
        ],
        compiler_params=_params("parallel", "arbitrary"),
        name="dilated_attn",
    )(z, z, z, z, z, *tables, *masks)


def _diff_kernel(q_ref, k_ref, v_ref, c_ref, sa_ref, sb_ref, lam_ref, sg_ref, o_ref, qb_ref, kb_ref, vb_ref,
                 score_a_ref, score_b_ref, *, lam_init):
    T = q_ref.shape[0]
    half = DIFF_DH // 8
    rope = lambda x: _rope(x, c_ref[...], sa_ref[...], sb_ref[...], half)
    qb_ref[...] = (rope(q_ref[...]) * (DIFF_DH ** -0.5 * LOG2E)).astype(BF16)
    kb_ref[...] = rope(k_ref[...]).astype(BF16)
    W = 2 * DIFF_DH
    vb_ref[:, :W] = v_ref[...].astype(BF16)
    vb_ref[:, W:] = jnp.ones((T, W), BF16)
    lam = lam_ref[...]
    lam_full = (jnp.exp(jnp.sum(lam[0:1] * lam[1:2], axis=-1, keepdims=True))
                - jnp.exp(jnp.sum(lam[2:3] * lam[3:4], axis=-1, keepdims=True)) + lam_init)
    first = lax.broadcasted_iota(jnp.int32, (1, LANES), 1) < DIFF_DH
    qblk = ATTN_Q_BLOCK

    nblk = T // qblk
    assert nblk % 2 == 0

    def block_rows(i):
        start = i * qblk
        return pl.ds(start if isinstance(start, int) else pl.multiple_of(start, qblk), qblk)

    def scores(i, s_ref):
        q = qb_ref[block_rows(i), :]
        zero = jnp.zeros_like(q)
        s_ref[0] = _dot_nt(jnp.where(first, q, zero), kb_ref[...])
        s_ref[1] = _dot_nt(jnp.where(first, zero, q), kb_ref[...])

    def attend(s):
        ox = _dot(jnp.exp2(s - jnp.max(s, axis=-1, keepdims=True)).astype(BF16), vb_ref[...])
        return ox[:, :W] / ox[:, W:]

    def finish(i, s_ref):
        o = attend(s_ref[0]) - lam_full * attend(s_ref[1])
        o_ref[block_rows(i), :] = (_rms(o, sg_ref[...]) * (1.0 - lam_init)).astype(o_ref.dtype)

    scores(0, score_a_ref)

    def pair(i, look_ahead):
        scores(i + 1, score_b_ref)
        finish(i, score_a_ref)
        if look_ahead:
            scores(i + 2, score_a_ref)
        finish(i + 1, score_b_ref)

    def body(j, carry):
        pair(2 * j, True)
        return carry

    lax.fori_loop(0, nblk // 2 - 1, body, 0)
    pair(nblk - 2, False)


def _diff(z, tables, lam, subln_g, lam_init, cols):
    B, T, _ = z.shape
    q0, k0, v0 = cols
    W = 2 * DIFF_DH
    zspec = lambda off: pl.BlockSpec((None, T, W), lambda b, h: (b, 0, off // W + h))
    tspec = pl.BlockSpec((T, LANES), lambda b, h: (0, 0))
    return pl.pallas_call(
        functools.partial(_diff_kernel, lam_init=lam_init),
        grid=(B, DIFF_HEADS),
        in_specs=[
            zspec(q0), zspec(k0), zspec(v0), tspec, tspec, tspec,
            pl.BlockSpec(lam.shape, lambda b, h: (0, 0)),
            pl.BlockSpec((1, W), lambda b, h: (0, 0)),
        ],
        out_specs=pl.BlockSpec((None, T, W), lambda b, h: (b, 0, h)),
        out_shape=jax.ShapeDtypeStruct((B, T, DIFF_HEADS * W), BF16),
        scratch_shapes=[pltpu.VMEM((T, W), BF16), pltpu.VMEM((T, W), BF16), pltpu.VMEM((T, 2 * W), BF16),
                        pltpu.VMEM((2, ATTN_Q_BLOCK, T), F32), pltpu.VMEM((2, ATTN_Q_BLOCK, T), F32)],
        compiler_params=_params("parallel", "arbitrary"),
        name="diff_attn",
    )(z, z, z, *tables, lam, subln_g.reshape(1, W))


_EVEN_MAIN = 2 * GLA_HEADS * GLA_DK + 2 * GLA_HEADS * GLA_DV
_EVEN_NA = 3 * NA_HEADS * HEAD_DIM
_EVEN_COLS_GLA = (0, GLA_HEADS * GLA_DK, 2 * GLA_HEADS * GLA_DK, 2 * GLA_HEADS * GLA_DK + GLA_HEADS * GLA_DV,
                  _EVEN_MAIN + _EVEN_NA)
_EVEN_COLS_NA = tuple(_EVEN_MAIN + i * NA_HEADS * HEAD_DIM for i in range(3))
_ODD_Q = len(DIL_PAIRS) * DIL_HEADS * HEAD_DIM
_ODD_COLS_DIL = (0, _ODD_Q, _ODD_Q + DIL_HEADS * HEAD_DIM)
_ODD_COLS_DIFF = tuple(_ODD_Q + 2 * DIL_HEADS * HEAD_DIM + i * DIFF_HEADS * 2 * DIFF_DH for i in range(3))


def _prepare_even(w_in, w_out, wg_f, bg_f, wg_b, bg_b, rpb, T):
    D = w_in.shape[0]
    lr0 = _EVEN_MAIN
    lr1 = lr0 + 2 * GLA_RANK
    w = jnp.concatenate([w_in[:, :lr0], w_in[:, lr1:], w_in[:, lr0:lr1],
                         jnp.zeros((D, MXU_WIDTH - 2 * GLA_RANK), w_in.dtype)], axis=1).astype(BF16)
    pad = lambda a, before: jnp.pad(a, ((before, LANES - GLA_RANK - before), (0, 0))).astype(BF16)
    ya_width = GLA_HEADS * GLA_DV
    return dict(w_in=w, wa=w_out[:ya_width].astype(BF16), wb=w_out[ya_width:].astype(BF16),
                wgf=pad(wg_f, 0), wgb=pad(wg_b, GLA_RANK),
                bgf=bg_f.reshape(1, -1).astype(F32), bgb=bg_b.reshape(1, -1).astype(F32),
                bias=_na_bias_table(rpb, T))


def _trunk(x, p, w):
    B, T, D = x.shape
    M = B * T
    h = x.reshape(M, D)
    for i in range(DEPTH):
        g = w["norm_g"][i]
        j = i // 2
        if i % 2 == 0:
            e = w["even"][j]
            z = _norm_matmul(h, g[0], e["w_in"], F32).reshape(B, T, -1)
            ya = _gla(z, e["wgf"], e["bgf"], e["wgb"], e["bgb"], w["gla_norm_g"][j].reshape(1, -1), _EVEN_COLS_GLA)
            yb = _na(z, e["bias"], _EVEN_COLS_NA)
            wa, wb = e["wa"], e["wb"]
        else:
            o = w["odd"][j]
            z = _norm_matmul(h, g[0], o["w_in"], F32).reshape(B, T, -1)
            ya = _dilated(z, w["rope_dil"], *w["dil_masks"], _ODD_COLS_DIL)
            yb = _diff(z, w["rope_diff"], w["diff_lambda"][j], w["diff_subln_g"][j],
                       0.8 - 0.6 * math.exp(-0.3 * i), _ODD_COLS_DIFF)
            wa, wb = o["wa"], o["wb"]
        h = _out_proj(h, ya.reshape(M, -1), yb.reshape(M, -1), g[1], wa, wb)
        h = _ffn(h, g[2], g[3], w["ffn_gate"][i], w["ffn_up"][i], w["ffn_down"][i])
        h = _ple(h, p[i].reshape(M, -1), g[4], w["ple_proj"][i], w["ple_gate"][i])
    return h.reshape(B, T, D)


def kernel(x_prompt, x_sample, p_prompt, p_sample, norm_g, w_in_even, w_out_even, gla_wg_fwd, gla_bg_fwd, gla_wg_bwd, gla_bg_bwd, gla_norm_g, na_rpb, w_in_odd, w_out_odd, diff_lambda, diff_subln_g, w_ffn_gate, w_ffn_up, w_ffn_down, w_ple_proj, w_ple_gate):
    T = x_prompt.shape[1]
    assert x_sample.shape[1] == T
    yc_width = DIL_HEADS * HEAD_DIM
    w = dict(
        norm_g=norm_g.astype(F32),
        even=[_prepare_even(w_in_even[j], w_out_even[j], gla_wg_fwd[j], gla_bg_fwd[j], gla_wg_bwd[j],
                            gla_bg_bwd[j], na_rpb[j], T) for j in range(w_in_even.shape[0])],
        odd=[dict(w_in=w_in_odd[j].astype(BF16), wa=w_out_odd[j, :yc_width].astype(BF16),
                  wb=w_out_odd[j, yc_width:].astype(BF16)) for j in range(w_in_odd.shape[0])],
        gla_norm_g=gla_norm_g.astype(F32),
        diff_lambda=diff_lambda.astype(F32),
        diff_subln_g=diff_subln_g.astype(F32),
        rope_dil=_rope_tables(T, ROT_DIM, HEAD_DIM),
        dil_masks=_dil_masks(T),
        rope_diff=_rope_tables(T, DIFF_DH // 4, DIFF_DH),
        ffn_gate=w_ffn_gate.astype(BF16), ffn_up=w_ffn_up.astype(BF16), ffn_down=w_ffn_down.astype(BF16),
        ple_proj=w_ple_proj.astype(BF16), ple_gate=w_ple_gate.astype(BF16),
    )
    return (_trunk(x_prompt, p_prompt, w), _trunk(x_sample, p_sample, w))
```

```python
import functools
import math

import jax
import jax.numpy as jnp
from jax import lax
from jax.experimental import pallas as pl
from jax.experimental.pallas import tpu as pltpu

F32 = jnp.float32
BF16 = jnp.bfloat16

D_MODEL = 2048
DEPTH = 2
HEAD_DIM = 128
PLE_DIM = 256
GRID_W = 64
EPS = 1e-6
NEG = -1e30
LOG2E = math.log2(math.e)
LN2 = math.log(2.0)
ROPE_THETA = 500000.0
ROT_DIM = HEAD_DIM // 4
D_FF = 5632
GLA_HEADS = 4
GLA_DV = 256
GLA_DK = 128
GLA_RANK = 16
GLA_TAU = 16.0
GLA_CHUNK = 64
NA_HEADS = 8
NA_WIN_ROWS = 8
NA_WIN_COLS = 16
DIL_PAIRS = ((128, 1), (512, 4), (2048, 16))
DIL_HEADS = 8
DIFF_DH = 64
DIFF_HEADS = 8

LANES = 128
MXU_WIDTH = 256
VMEM_LIMIT = 56 * 1024 * 1024

TOKEN_TILE = 512
FFN_TILE = 512
IN_PROJ_TILE = 1280
IN_PROJ_ROWS = 1024
GLA_BLOCK = 256
ATTN_Q_BLOCK = 512
DIL_Q_BLOCK = 128
NA_GROUP = 4
NA_GROUP_WIN = NA_WIN_ROWS + NA_GROUP - 1

NT_DIMS = (((1,), (1,)), ((), ()))
TN_DIMS = (((0,), (0,)), ((), ()))


def _params(*semantics):
    return pltpu.CompilerParams(dimension_semantics=semantics, vmem_limit_bytes=VMEM_LIMIT)


def _divisor_tile(n, cap):
    return max(t for t in range(MXU_WIDTH, cap + 1, MXU_WIDTH) if n % t == 0)


def _rms(x, g):
    return x * lax.rsqrt(jnp.mean(x * x, axis=-1, keepdims=True) + EPS) * g


def _sigmoid(x):
    return 1.0 / (1.0 + jnp.exp(-x))


def _dot(a, b):
    return jnp.dot(a, b, preferred_element_type=F32)


def _dot_nt(a, b):
    return lax.dot_general(a, b, NT_DIMS, preferred_element_type=F32)


def _norm_matmul_kernel(x_ref, g_ref, w_ref, o_ref, a_ref):
    @pl.when(pl.program_id(1) == 0)
    def _():
        a_ref[...] = _rms(x_ref[...], g_ref[...]).astype(BF16)

    o_ref[...] = _dot(a_ref[...], w_ref[...]).astype(o_ref.dtype)


def _norm_matmul(x, g, w, out_dtype):
    M, D = x.shape
    N = w.shape[1]
    tm, tn = IN_PROJ_ROWS, _divisor_tile(N, IN_PROJ_TILE)
    assert M % tm == 0
    return pl.pallas_call(
        _norm_matmul_kernel,
        grid=(M // tm, N // tn),
        in_specs=[
            pl.BlockSpec((tm, D), lambda i, j: (i, 0)),
            pl.BlockSpec((1, D), lambda i, j: (0, 0)),
            pl.BlockSpec((D, tn), lambda i, j: (0, j)),
        ],
        out_specs=pl.BlockSpec((tm, tn), lambda i, j: (i, j)),
        out_shape=jax.ShapeDtypeStruct((M, N), out_dtype),
        scratch_shapes=[pltpu.VMEM((tm, D), BF16)],
        compiler_params=_params("parallel", "arbitrary"),
        name="norm_in_proj",
    )(x, g.reshape(1, D), w)


def _out_proj_kernel(h_ref, ya_ref, yb_ref, g_ref, wa_ref, wb_ref, o_ref):
    m = _dot(ya_ref[...], wa_ref[...]) + _dot(yb_ref[...], wb_ref[...])
    o_ref[...] = h_ref[...] + _rms(m, g_ref[...])


def _out_proj(h, ya, yb, g, wa, wb):
    M, D = h.shape
    Ka, Kb = ya.shape[1], yb.shape[1]
    tm = TOKEN_TILE
    return pl.pallas_call(
        _out_proj_kernel,
        grid=(M // tm,),
        in_specs=[
            pl.BlockSpec((tm, D), lambda i: (i, 0)),
            pl.BlockSpec((tm, Ka), lambda i: (i, 0)),
            pl.BlockSpec((tm, Kb), lambda i: (i, 0)),
            pl.BlockSpec((1, D), lambda i: (0, 0)),
            pl.BlockSpec((Ka, D), lambda i: (0, 0)),
            pl.BlockSpec((Kb, D), lambda i: (0, 0)),
        ],
        out_specs=pl.BlockSpec((tm, D), lambda i: (i, 0)),
        out_shape=jax.ShapeDtypeStruct((M, D), F32),
        compiler_params=_params("parallel"),
        name="out_proj",
    )(h, ya, yb, g.reshape(1, D), wa, wb)


def _ffn_kernel(h_ref, g2_ref, g3_ref, wg_ref, wu_ref, wd_ref, o_ref, a_ref, acc_ref):
    f = pl.program_id(1)

    @pl.when(f == 0)
    def _():
        a_ref[...] = _rms(h_ref[...], g2_ref[...]).astype(BF16)
        acc_ref[...] = jnp.zeros_like(acc_ref)

    a = a_ref[...]
    gate = _dot(a, wg_ref[...])
    up = _dot(a, wu_ref[...])
    act = (gate * _sigmoid(gate) * up).astype(BF16)
    acc_ref[...] += _dot(act, wd_ref[...])

    @pl.when(f == pl.num_programs(1) - 1)
    def _():
        o_ref[...] = h_ref[...] + _rms(acc_ref[...], g3_ref[...])


def _ffn(h, g2, g3, wg, wu, wd):
    M, D = h.shape
    F = wg.shape[1]
    tm, tf = TOKEN_TILE, FFN_TILE
    assert M % tm == 0 and F % tf == 0
    return pl.pallas_call(
        _ffn_kernel,
        grid=(M // tm, F // tf),
        in_specs=[
            pl.BlockSpec((tm, D), lambda i, f: (i, 0)),
            pl.BlockSpec((1, D), lambda i, f: (0, 0)),
            pl.BlockSpec((1, D), lambda i, f: (0, 0)),
            pl.BlockSpec((D, tf), lambda i, f: (0, f)),
            pl.BlockSpec((D, tf), lambda i, f: (0, f)),
            pl.BlockSpec((tf, D), lambda i, f: (f, 0)),
        ],
        out_specs=pl.BlockSpec((tm, D), lambda i, f: (i, 0)),
        out_shape=jax.ShapeDtypeStruct((M, D), F32),
        scratch_shapes=[pltpu.VMEM((tm, D), BF16), pltpu.VMEM((tm, D), F32)],
        compiler_params=_params("parallel", "arbitrary"),
        name="ffn",
    )(h, g2.reshape(1, D), g3.reshape(1, D), wg, wu, wd)


def _ple_kernel(h_ref, p_ref, g_ref, wp_ref, wgate_ref, o_ref):
    h = h_ref[...]
    e = _rms(_dot(p_ref[...].astype(BF16), wp_ref[...]), g_ref[...])
    gate = _sigmoid(_dot(h.astype(BF16), wgate_ref[...]))
    o_ref[...] = h + e * gate


def _ple(h, p, g, wp, wgate):
    M, D = h.shape
    P = p.shape[1]
    tm = TOKEN_TILE
    return pl.pallas_call(
        _ple_kernel,
        grid=(M // tm,),
        in_specs=[
            pl.BlockSpec((tm, D), lambda i: (i, 0)),
            pl.BlockSpec((tm, P), lambda i: (i, 0)),
            pl.BlockSpec((1, D), lambda i: (0, 0)),
            pl.BlockSpec((P, D), lambda i: (0, 0)),
            pl.BlockSpec((D, D), lambda i: (0, 0)),
        ],
        out_specs=pl.BlockSpec((tm, D), lambda i: (i, 0)),
        out_shape=jax.ShapeDtypeStruct((M, D), F32),
        compiler_params=_params("parallel"),
        name="ple",
    )(h, p, g.reshape(1, D), wp, wgate)


def _split3(g):
    g1 = g.astype(BF16)
    r1 = g - g1.astype(F32)
    g2 = r1.astype(BF16)
    g3 = (r1 - g2.astype(F32)).astype(BF16)
    return g1, g2, g3


def _mask_dot(mask, parts):
    m = mask.astype(BF16)
    return _dot(m, parts[0]) + _dot(m, parts[1]) + _dot(m, parts[2])


def _log_sigmoid(x):
    return jnp.minimum(x, 0.0) - jnp.log1p(jnp.exp(-jnp.abs(x)))


def _gla_kernel(q_ref, k_ref, v_ref, r_ref, lr_ref, wgf_ref, bgf_ref, wgb_ref, bgb_ref, gn_ref, o_ref,
                vb_ref, intra_ref, gate_ref, cum_ref, kd_ref, qdf_ref, kuf_ref, totf_ref, interf_ref, sf_ref,
                qdb_ref, kub_ref, totb_ref, interb_ref, sb_ref):
    T = q_ref.shape[0]
    L = GLA_CHUNK
    N = T // L
    R = GLA_BLOCK
    row = lax.broadcasted_iota(jnp.int32, (R, R), 0)
    col = lax.broadcasted_iota(jnp.int32, (R, R), 1)
    same_chunk = (row // L) == (col // L)
    scale = GLA_DK ** -0.5
    vb_ref[...] = v_ref[...].astype(BF16)
    fwd = (wgf_ref, bgf_ref, same_chunk & (col <= row), same_chunk & (col <= row), L - 1, qdf_ref, kuf_ref, totf_ref)
    bwd = (wgb_ref, bgb_ref, same_chunk & (col >= row), same_chunk & (col > row), 0, qdb_ref, kub_ref, totb_ref)

    blocks = [pl.ds(i * R, R) for i in range(T // R)]
    for rows in blocks:
        lr = lr_ref[rows, :].astype(BF16)
        for d, (w_ref, b_ref, *_) in enumerate((fwd, bwd)):
            pieces = _split3(_log_sigmoid(_dot(lr, w_ref[...]) + b_ref[...]) / GLA_TAU)
            for piece_index, piece in enumerate(pieces):
                gate_ref[d, piece_index, rows, :] = piece
    for rows in blocks:
        for d, (_, _, cum_mask, *_) in enumerate((fwd, bwd)):
            cum_ref[d, rows, :] = _mask_dot(cum_mask, [gate_ref[d, p, rows, :] for p in range(3)])
    for rows in blocks:
        q = q_ref[rows, :] * scale
        k = k_ref[rows, :]
        for d, (_, _, _, _, last_row, qd_ref, ku_ref, tot_ref) in enumerate((fwd, bwd)):
            b = cum_ref[d, rows, :]
            b3 = b.reshape(R // L, L, GLA_DK)
            tot = jnp.broadcast_to(b3[:, last_row:last_row + 1, :], b3.shape).reshape(R, GLA_DK)
            qd_ref[rows, :] = (q * jnp.exp(b)).astype(BF16)
            kd_ref[d, rows, :] = (k * jnp.exp(-b)).astype(BF16)
            ku_ref[rows, :] = (k * jnp.exp(tot - b)).astype(BF16)
            tot_ref[rows, :] = tot
    for rows in blocks:
        v = vb_ref[rows, :]
        intra = None
        for d, (_, _, _, att_mask, _, qd_ref, _, _) in enumerate((fwd, bwd)):
            att = jnp.where(att_mask, _dot_nt(qd_ref[rows, :], kd_ref[d, rows, :]), 0.0).astype(BF16)
            intra = _dot(att, v) if intra is None else intra + _dot(att, v)
        intra_ref[rows, :] = intra

    sf_ref[...] = jnp.zeros_like(sf_ref)
    sb_ref[...] = jnp.zeros_like(sb_ref)

    def scan(n, carry):
        for idx, qd_ref, ku_ref, tot_ref, inter_ref, s_ref in (
                (n, qdf_ref, kuf_ref, totf_ref, interf_ref, sf_ref),
                (N - 1 - n, qdb_ref, kub_ref, totb_ref, interb_ref, sb_ref)):
            start = pl.multiple_of(idx * L, L)
            rows = pl.ds(start, L)
            state = s_ref[...]
            inter_ref[rows, :] = _dot_nt(qd_ref[rows, :], state.astype(BF16))
            update = lax.dot_general(vb_ref[rows, :], ku_ref[rows, :], TN_DIMS, preferred_element_type=F32)
            s_ref[...] = state * jnp.exp(tot_ref[pl.ds(start, 1), :]) + update
        return carry

    lax.fori_loop(0, N, scan, 0, unroll=True)

    r = r_ref[...]
    o = _rms(intra_ref[...] + interf_ref[...] + interb_ref[...], gn_ref[...])
    o_ref[...] = (o * (r * _sigmoid(r))).astype(o_ref.dtype)


def _gla(z, wgf, bgf, wgb, bgb, gn, cols):
    B, T, _ = z.shape
    dk, dv = GLA_DK, GLA_DV
    q0, k0, v0, r0, lr0 = cols
    zspec = lambda width, off: pl.BlockSpec((None, T, width), lambda b, h: (b, 0, off // width + h))
    return pl.pallas_call(
        _gla_kernel,
        grid=(B, GLA_HEADS),
        in_specs=[
            zspec(dk, q0), zspec(dk, k0), zspec(dv, v0), zspec(dv, r0),
            pl.BlockSpec((None, T, LANES), lambda b, h: (b, 0, lr0 // LANES)),
            pl.BlockSpec((LANES, dk), lambda b, h: (0, h)),
            pl.BlockSpec((1, dk), lambda b, h: (0, h)),
            pl.BlockSpec((LANES, dk), lambda b, h: (0, h)),
            pl.BlockSpec((1, dk), lambda b, h: (0, h)),
            pl.BlockSpec((1, dv), lambda b, h: (0, 0)),
        ],
        out_specs=pl.BlockSpec((None, T, dv), lambda b, h: (b, 0, h)),
        out_shape=jax.ShapeDtypeStruct((B, T, GLA_HEADS * dv), BF16),
        scratch_shapes=[pltpu.VMEM((T, dv), BF16), pltpu.VMEM((T, dv), F32), pltpu.VMEM((2, 3, T, dk), BF16),
                        pltpu.VMEM((2, T, dk), F32), pltpu.VMEM((2, T, dk), BF16)] + 2 * [
            pltpu.VMEM((T, dk), BF16), pltpu.VMEM((T, dk), BF16), pltpu.VMEM((T, dk), F32),
            pltpu.VMEM((T, dv), F32), pltpu.VMEM((dv, dk), F32),
        ],
        compiler_params=_params("parallel", "arbitrary"),
        name="gla",
    )(z, z, z, z, z, wgf, bgf, wgb, bgb, gn)


def _na_window_start(gi, rows):
    lo, hi = 0, rows - NA_GROUP_WIN
    start = NA_GROUP * gi - NA_WIN_ROWS // 2
    return min(max(start, lo), hi) if isinstance(gi, int) else jnp.clip(start, lo, hi)


def _na_kernel(q_ref, k_ref, v_ref, bias_ref, o_ref, kb_ref, vx_ref):
    T = q_ref.shape[0]
    rows = T // GRID_W
    groups = rows // NA_GROUP
    qlen = NA_GROUP * GRID_W
    klen = NA_GROUP_WIN * GRID_W
    kb_ref[...] = k_ref[...].astype(BF16)
    vx_ref[:, :HEAD_DIM] = v_ref[...].astype(BF16)
    vx_ref[:, HEAD_DIM:] = jnp.ones((T, HEAD_DIM), BF16)
    scale = HEAD_DIM ** -0.5 * LOG2E

    def body(gi, carry):
        variant = jnp.where(gi == 0, 0, jnp.where(gi == groups - 1, 2, 1))
        qrows = pl.ds(pl.multiple_of(gi * qlen, qlen), qlen)
        krows = pl.ds(pl.multiple_of(_na_window_start(gi, rows) * GRID_W, GRID_W), klen)
        s = _dot_nt(q_ref[qrows, :].astype(BF16), kb_ref[krows, :]) * scale + bias_ref[variant]
        e = jnp.exp2(s - jnp.max(s, axis=-1, keepdims=True)).astype(BF16)
        ox = _dot(e, vx_ref[krows, :])
        o_ref[qrows, :] = (ox[:, :HEAD_DIM] / ox[:, HEAD_DIM:]).astype(o_ref.dtype)
        return carry

    lax.fori_loop(0, groups, body, 0, unroll=True)


def _na(z, bias, cols):
    B, T, _ = z.shape
    q0, k0, v0 = cols
    zspec = lambda off: pl.BlockSpec((None, T, HEAD_DIM), lambda b, h: (b, 0, off // HEAD_DIM + h))
    return pl.pallas_call(
        _na_kernel,
        grid=(B, NA_HEADS),
        in_specs=[
            zspec(q0), zspec(k0), zspec(v0),
            pl.BlockSpec((None,) + bias.shape[1:], lambda b, h: (h, 0, 0, 0)),
        ],
        out_specs=pl.BlockSpec((None, T, HEAD_DIM), lambda b, h: (b, 0, h)),
        out_shape=jax.ShapeDtypeStruct((B, T, NA_HEADS * HEAD_DIM), BF16),
        scratch_shapes=[pltpu.VMEM((T, HEAD_DIM), BF16), pltpu.VMEM((T, 2 * HEAD_DIM), BF16)],
        compiler_params=_params("parallel", "arbitrary"),
        name="neighbourhood_attn",
    )(z, z, z, bias)


def _na_bias_table(rpb, T):
    H = rpb.shape[0]
    W = GRID_W
    rows = T // W
    groups = rows // NA_GROUP
    assert rows % NA_GROUP == 0 and rows >= NA_GROUP_WIN and groups >= 3
    c = jnp.arange(W)
    c0 = jnp.clip(c - NA_WIN_COLS // 2, 0, W - NA_WIN_COLS)
    col_ok = (c[None, :] >= c0[:, None]) & (c[None, :] < c0[:, None] + NA_WIN_COLS)
    dc = c[None, :] - c[:, None] + NA_WIN_COLS - 1
    onehot = (dc[None] == jnp.arange(2 * NA_WIN_COLS - 1)[:, None, None]).astype(F32)
    by_dr = jnp.einsum("hrj,jqk->hrqk", rpb.astype(F32), onehot, precision=lax.Precision.HIGHEST)
    by_dr = jnp.where(col_ok, by_dr * LOG2E, NEG)
    masked = jnp.full((H, W, W), NEG, F32)
    variants = []
    for gi in (0, 1, groups - 1):
        ws = _na_window_start(gi, rows)
        per_row = []
        for ri in range(NA_GROUP):
            r = NA_GROUP * gi + ri
            kr0 = min(max(r - NA_WIN_ROWS // 2, 0), rows - NA_WIN_ROWS)
            blocks = []
            for o in range(NA_GROUP_WIN):
                key_row = ws + o
                ok = kr0 <= key_row < kr0 + NA_WIN_ROWS
                blocks.append(by_dr[:, key_row - r + NA_WIN_ROWS - 1] if ok else masked)
            per_row.append(jnp.concatenate(blocks, axis=-1))
        variants.append(jnp.concatenate(per_row, axis=1))
    return jnp.stack(variants, axis=1)


def _rope_tables(T, rot, period):
    half = rot // 2
    inv = ROPE_THETA ** (-jnp.arange(half, dtype=F32) / half)
    ang = jnp.arange(T, dtype=F32)[:, None] * inv[None, :]
    cos, sin = jnp.cos(ang), jnp.sin(ang)
    zeros = jnp.zeros((T, half), F32)
    rest0 = jnp.zeros((T, period - rot), F32)
    c = jnp.concatenate([cos, cos, jnp.ones((T, period - rot), F32)], axis=1)
    sa = jnp.concatenate([zeros, sin, rest0], axis=1)
    sb = jnp.concatenate([-sin, zeros, rest0], axis=1)
    rep = LANES // period
    return tuple(jnp.tile(t, (1, rep)) for t in (c, sa, sb))


def _rope(x, c, sa, sb, half):
    return x * c + pltpu.roll(x, half, 1) * sa + pltpu.roll(x, LANES - half, 1) * sb


def _class_rows(x, n, dil):
    return pl.ds(x, n) if dil == 1 else pl.ds(x, n, stride=dil)


def _dil_block_plan(T):
    qb = DIL_Q_BLOCK
    plan = []
    for win, dil in DIL_PAIRS:
        radius = win // (2 * dil)
        n = T // dil
        if n >= qb:
            kw = min(n, qb + 2 * radius)
            blocks = []
            for q_start in range(0, T, qb):
                seg = (q_start // n) * n
                k_start = min(max(q_start - radius, seg), seg + n - kw)
                blocks.append((q_start, k_start, k_start - q_start))
        else:
            kw = qb
            blocks = [(q_start, q_start, 0) for q_start in range(0, T, qb)]
        plan.append(dict(dil=dil, n=n, radius=radius, kw=kw, blocks=blocks))
    return plan


def _dil_masks(T):
    qb = DIL_Q_BLOCK
    tables, index = [], {}
    for g, grp in enumerate(_dil_block_plan(T)):
        r = jnp.arange(qb)[:, None]
        c = jnp.arange(grp["kw"])[None, :]
        for _, _, off in grp["blocks"]:
            if (g, off) in index:
                continue
            ok = jnp.abs(c + off - r) <= grp["radius"]
            if grp["n"] < qb:
                ok = ok & (c // grp["n"] == r // grp["n"])
            index[(g, off)] = len(tables)
            tables.append(jnp.where(ok, 0.0, NEG).astype(F32))
    return tables, index


def _dil_kernel(*refs, T, mask_index):
    n_masks = len(set(mask_index.values()))
    q_refs = refs[:3]
    k_ref, v_ref, c_ref, sa_ref, sb_ref = refs[3:8]
    mask_refs = refs[8:8 + n_masks]
    o_ref, kr_ref, qr_ref, qd_all_ref, kd_all_ref, vx_all_ref, on_ref, ln_ref = refs[8 + n_masks:]
    half = ROT_DIM // 2
    qb = DIL_Q_BLOCK
    rope = lambda x: _rope(x, c_ref[...], sa_ref[...], sb_ref[...], half)
    kr_ref[...] = rope(k_ref[...])
    vx_all_ref[:, :, HEAD_DIM:] = jnp.ones((len(DIL_PAIRS), T, HEAD_DIM), BF16)
    scale = HEAD_DIM ** -0.5 * LOG2E

    def scatter(ref, g, start, value, grp):
        n, dil = grp["n"], grp["dil"]
        pos = start
        while pos < start + value.shape[0]:
            x, i = divmod(pos, n)
            take = min(n - i, start + value.shape[0] - pos)
            ref[g, _class_rows(x + i * dil, take, dil), :] = value[pos - start:pos - start + take]
            pos += take

    for g, grp in enumerate(_dil_block_plan(T)):
        n, dil, kw = grp["n"], grp["dil"], grp["kw"]
        qd_ref, kd_ref, vx_ref = qd_all_ref.at[g], kd_all_ref.at[g], vx_all_ref.at[g]
        qr_ref[...] = rope(q_refs[g][...]) * scale
        for x in range(dil):
            src = _class_rows(x, n, dil)
            dst = pl.ds(x * n, n)
            qd_ref[dst, :] = qr_ref[src, :].astype(BF16)
            kd_ref[dst, :] = kr_ref[src, :].astype(BF16)
            vx_ref[dst, :HEAD_DIM] = v_ref[src, :].astype(BF16)
        for q_start, k_start, off in grp["blocks"]:
            krows = pl.ds(k_start, kw)
            s = _dot_nt(qd_ref[pl.ds(q_start, qb), :], kd_ref[krows, :]) + mask_refs[mask_index[(g, off)]][...]
            m = jnp.max(s, axis=-1, keepdims=True)
            ox = _dot(jnp.exp2(s - m).astype(BF16), vx_ref[krows, :])
            l = ox[:, HEAD_DIM:]
            scatter(on_ref, g, q_start, ox[:, :HEAD_DIM] / l, grp)
            scatter(ln_ref, g, q_start, m * LN2 + jnp.log(l), grp)

    l0, l1, l2 = ln_ref[0], ln_ref[1], ln_ref[2]
    m = jnp.maximum(jnp.maximum(l0, l1), l2)
    e0, e1, e2 = jnp.exp(l0 - m), jnp.exp(l1 - m), jnp.exp(l2 - m)
    den = e0 + e1 + e2
    o = (e0 / den) * on_ref[0] + (e1 / den) * on_ref[1] + (e2 / den) * on_ref[2]
    o_ref[...] = o.astype(o_ref.dtype)


def _dilated(z, tables, masks, mask_index, cols):
    B, T, _ = z.shape
    q0, k0, v0 = cols
    G = len(DIL_PAIRS)
    assert G == 3 and all(T % (dil * GLA_CHUNK) == 0 for _, dil in DIL_PAIRS) and T % DIL_Q_BLOCK == 0
    zspec = lambda off: pl.BlockSpec((None, T, HEAD_DIM), lambda b, h: (b, 0, off // HEAD_DIM + h))
    const = lambda a: pl.BlockSpec(a.shape, lambda b, h: (0,) * a.ndim)
    return pl.pallas_call(
        functools.partial(_dil_kernel, T=T, mask_index=mask_index),
        grid=(B, DIL_HEADS),
        in_specs=([zspec(q0 + g * DIL_HEADS * HEAD_DIM) for g in range(G)] + [zspec(k0), zspec(v0)]
                  + [const(t) for t in tables] + [const(m) for m in masks]),
        out_specs=pl.BlockSpec((None, T, HEAD_DIM), lambda b, h: (b, 0, h)),
        out_shape=jax.ShapeDtypeStruct((B, T, DIL_HEADS * HEAD_DIM), BF16),
        scratch_shapes=[
            pltpu.VMEM((T, HEAD_DIM), F32), pltpu.VMEM((T, HEAD_DIM), F32),
            pltpu.VMEM((G, T, HEAD_DIM), BF16), pltpu.VMEM((G, T, HEAD_DIM), BF16),
            pltpu.VMEM((G, T, 2 * HEAD_DIM), BF16),
            pltpu.VMEM((G, T, HEAD_DIM), F32), pltpu.VMEM((G, T, LANES), F32),
        ],
        compiler_params=_params("parallel", "arbitrary"),
        name="dilated_attn",
    )(z, z, z, z, z, *tables, *masks)


def _diff_kernel(q_ref, k_ref, v_ref, c_ref, sa_ref, sb_ref, lam_ref, sg_ref, o_ref, qb_ref, kb_ref, vb_ref,
                 score_a_ref, score_b_ref, *, lam_init):
    T = q_ref.shape[0]
    half = DIFF_DH // 8
    rope = lambda x: _rope(x, c_ref[...], sa_ref[...], sb_ref[...], half)
    qb_ref[...] = (rope(q_ref[...]) * (DIFF_DH ** -0.5 * LOG2E)).astype(BF16)
    kb_ref[...] = rope(k_ref[...]).astype(BF16)
    W = 2 * DIFF_DH
    vb_ref[:, :W] = v_ref[...].astype(BF16)
    vb_ref[:, W:] = jnp.ones((T, W), BF16)
    lam = lam_ref[...]
    lam_full = (jnp.exp(jnp.sum(lam[0:1] * lam[1:2], axis=-1, keepdims=True))
                - jnp.exp(jnp.sum(lam[2:3] * lam[3:4], axis=-1, keepdims=True)) + lam_init)
    first = lax.broadcasted_iota(jnp.int32, (1, LANES), 1) < DIFF_DH
    qblk = ATTN_Q_BLOCK

    nblk = T // qblk
    assert nblk % 2 == 0

    def block_rows(i):
        start = i * qblk
        return pl.ds(start if isinstance(start, int) else pl.multiple_of(start, qblk), qblk)

    def scores(i, s_ref):
        q = qb_ref[block_rows(i), :]
        zero = jnp.zeros_like(q)
        s_ref[0] = _dot_nt(jnp.where(first, q, zero), kb_ref[...])
        s_ref[1] = _dot_nt(jnp.where(first, zero, q), kb_ref[...])

    def attend(s):
        ox = _dot(jnp.exp2(s - jnp.max(s, axis=-1, keepdims=True)).astype(BF16), vb_ref[...])
        return ox[:, :W] / ox[:, W:]

    def finish(i, s_ref):
        o = attend(s_ref[0]) - lam_full * attend(s_ref[1])
        o_ref[block_rows(i), :] = (_rms(o, sg_ref[...]) * (1.0 - lam_init)).astype(o_ref.dtype)

    scores(0, score_a_ref)

    def pair(i, look_ahead):
        scores(i + 1, score_b_ref)
        finish(i, score_a_ref)
        if look_ahead:
            scores(i + 2, score_a_ref)
        finish(i + 1, score_b_ref)

    def body(j, carry):
        pair(2 * j, True)
        return carry

    lax.fori_loop(0, nblk // 2 - 1, body, 0)
    pair(nblk - 2, False)


def _diff(z, tables, lam, subln_g, lam_init, cols):
    B, T, _ = z.shape
    q0, k0, v0 = cols
    W = 2 * DIFF_DH
    zspec = lambda off: pl.BlockSpec((None, T, W), lambda b, h: (b, 0, off // W + h))
    tspec = pl.BlockSpec((T, LANES), lambda b, h: (0, 0))
    return pl.pallas_call(
        functools.partial(_diff_kernel, lam_init=lam_init),
        grid=(B, DIFF_HEADS),
        in_specs=[
            zspec(q0), zspec(k0), zspec(v0), tspec, tspec, tspec,
            pl.BlockSpec(lam.shape, lambda b, h: (0, 0)),
            pl.BlockSpec((1, W), lambda b, h: (0, 0)),
        ],
        out_specs=pl.BlockSpec((None, T, W), lambda b, h: (b, 0, h)),
        out_shape=jax.ShapeDtypeStruct((B, T, DIFF_HEADS * W), BF16),
        scratch_shapes=[pltpu.VMEM((T, W), BF16), pltpu.VMEM((T, W), BF16), pltpu.VMEM((T, 2 * W), BF16),
                        pltpu.VMEM((2, ATTN_Q_BLOCK, T), F32), pltpu.VMEM((2, ATTN_Q_BLOCK, T), F32)],
        compiler_params=_params("parallel", "arbitrary"),
        name="diff_attn",
    )(z, z, z, *tables, lam, subln_g.reshape(1, W))


_EVEN_MAIN = 2 * GLA_HEADS * GLA_DK + 2 * GLA_HEADS * GLA_DV
_EVEN_NA = 3 * NA_HEADS * HEAD_DIM
_EVEN_COLS_GLA = (0, GLA_HEADS * GLA_DK, 2 * GLA_HEADS * GLA_DK, 2 * GLA_HEADS * GLA_DK + GLA_HEADS * GLA_DV,
                  _EVEN_MAIN + _EVEN_NA)
_EVEN_COLS_NA = tuple(_EVEN_MAIN + i * NA_HEADS * HEAD_DIM for i in range(3))
_ODD_Q = len(DIL_PAIRS) * DIL_HEADS * HEAD_DIM
_ODD_COLS_DIL = (0, _ODD_Q, _ODD_Q + DIL_HEADS * HEAD_DIM)
_ODD_COLS_DIFF = tuple(_ODD_Q + 2 * DIL_HEADS * HEAD_DIM + i * DIFF_HEADS * 2 * DIFF_DH for i in range(3))


def _prepare_even(w_in, w_out, wg_f, bg_f, wg_b, bg_b, rpb, T):
    D = w_in.shape[0]
    lr0 = _EVEN_MAIN
    lr1 = lr0 + 2 * GLA_RANK
    w = jnp.concatenate([w_in[:, :lr0], w_in[:, lr1:], w_in[:, lr0:lr1],
                         jnp.zeros((D, MXU_WIDTH - 2 * GLA_RANK), w_in.dtype)], axis=1).astype(BF16)
    pad = lambda a, before: jnp.pad(a, ((before, LANES - GLA_RANK - before), (0, 0))).astype(BF16)
    ya_width = GLA_HEADS * GLA_DV
    return dict(w_in=w, wa=w_out[:ya_width].astype(BF16), wb=w_out[ya_width:].astype(BF16),
                wgf=pad(wg_f, 0), wgb=pad(wg_b, GLA_RANK),
                bgf=bg_f.reshape(1, -1).astype(F32), bgb=bg_b.reshape(1, -1).astype(F32),
                bias=_na_bias_table(rpb, T))


def _trunk(x, p, w):
    B, T, D = x.shape
    M = B * T
    h = x.reshape(M, D)
    for i in range(DEPTH):
        g = w["norm_g"][i]
        j = i // 2
        if i % 2 == 0:
            e = w["even"][j]
            z = _norm_matmul(h, g[0], e["w_in"], F32).reshape(B, T, -1)
            ya = _gla(z, e["wgf"], e["bgf"], e["wgb"], e["bgb"], w["gla_norm_g"][j].reshape(1, -1), _EVEN_COLS_GLA)
            yb = _na(z, e["bias"], _EVEN_COLS_NA)
            wa, wb = e["wa"], e["wb"]
        else:
            o = w["odd"][j]
            z = _norm_matmul(h, g[0], o["w_in"], F32).reshape(B, T, -1)
            ya = _dilated(z, w["rope_dil"], *w["dil_masks"], _ODD_COLS_DIL)
            yb = _diff(z, w["rope_diff"], w["diff_lambda"][j], w["diff_subln_g"][j],
                       0.8 - 0.6 * math.exp(-0.3 * i), _ODD_COLS_DIFF)
            wa, wb = o["wa"], o["wb"]
        h = _out_proj(h, ya.reshape(M, -1), yb.reshape(M, -1), g[1], wa, wb)
        h = _ffn(h, g[2], g[3], w["ffn_gate"][i], w["ffn_up"][i], w["ffn_down"][i])
        h = _ple(h, p[i].reshape(M, -1), g[4], w["ple_proj"][i], w["ple_gate"][i])
    return h.reshape(B, T, D)


def kernel(x_prompt, x_sample, p_prompt, p_sample, norm_g, w_in_even, w_out_even, gla_wg_fwd, gla_bg_fwd, gla_wg_bwd, gla_bg_bwd, gla_norm_g, na_rpb, w_in_odd, w_out_odd, diff_lambda, diff_subln_g, w_ffn_gate, w_ffn_up, w_ffn_down, w_ple_proj, w_ple_gate):
    T = x_prompt.shape[1]
    assert x_sample.shape[1] == T
    yc_width = DIL_HEADS * HEAD_DIM
    w = dict(
        norm_g=norm_g.astype(F32),
        even=[_prepare_even(w_in_even[j], w_out_even[j], gla_wg_fwd[j], gla_bg_fwd[j], gla_wg_bwd[j],
                            gla_bg_bwd[j], na_rpb[j], T) for j in range(w_in_even.shape[0])],
        odd=[dict(w_in=w_in_odd[j].astype(BF16), wa=w_out_odd[j, :yc_width].astype(BF16),
                  wb=w_out_odd[j, yc_width:].astype(BF16)) for j in range(w_in_odd.shape[0])],
        gla_norm_g=gla_norm_g.astype(F32),
        diff_lambda=diff_lambda.astype(F32),
        diff_subln_g=diff_subln_g.astype(F32),
        rope_dil=_rope_tables(T, ROT_DIM, HEAD_DIM),
        dil_masks=_dil_masks(T),
        rope_diff=_rope_tables(T, DIFF_DH // 4, DIFF_DH),
        ffn_gate=w_ffn_gate.astype(BF16), ffn_up=w_ffn_up.astype(BF16), ffn_down=w_ffn_down.astype(BF16),
        ple_proj=w_ple_proj.astype(BF16), ple_gate=w_ple_gate.astype(BF16),
    )
    return (_trunk(x_prompt, p_prompt, w), _trunk(x_sample, p_sample, w))
```
